```python
import jax, jax.numpy as jnp
from jax import lax
import numpy as np

D_MODEL = 2048
BATCH = 4
SEQ = 2048
DEPTH = 4
DEC_BATCH = 128
DEC_SEQ = 1
PAST_LEN = 8192
PAGE_SIZE = 128

N_MIXERS = 3
N_HEADS = 16
QK_NOPE_DIM = 128
QK_ROPE_DIM = 64
V_HEAD_DIM = 128
Q_LORA_RANK = 512
KV_LORA_RANK = 512
ROPE_THETA = 10000.0
Q_BLOCK = 128
ATTN_SCALE = (QK_NOPE_DIM + QK_ROPE_DIM) ** -0.5
CONV_WIDTH = 31
D_CONV = D_MODEL
CHUNK = 128
D_SGU = 2 * D_MODEL
N_SGU_GROUPS = 16
SGU_GROUP_DIM = D_SGU // N_SGU_GROUPS
DENSE_FF = 5632
N_EXPERTS = 8
TOP_K = 2
EXPERT_FF = 7168
N_MOD = 6
ALPHA = (2 * DEPTH) ** 0.25
BETA = (8 * DEPTH) ** -0.25
LN_EPS = 1e-5
RMS_EPS = 1e-6

N_A_LAYERS = (DEPTH + 2) // 3
N_B_LAYERS = (DEPTH + 1) // 3
N_C_LAYERS = DEPTH // 3
N_DENSE_LAYERS = (DEPTH + 1) // 2
N_MOE_LAYERS = DEPTH // 2

kernel_name = 'hybrid_mla_conv_sgu_deepnorm_adaln_step'


def layer_norm(x, g, b):
    xf = x.astype(jnp.float32)
    mu = jnp.mean(xf, axis=-1, keepdims=True)
    var = jnp.mean(jnp.square(xf - mu), axis=-1, keepdims=True)
    return ((xf - mu) * lax.rsqrt(var + LN_EPS)).astype(x.dtype) * g + b


def rms_norm(x, g):
    xf = x.astype(jnp.float32)
    return (xf * lax.rsqrt(jnp.mean(xf * xf, axis=-1, keepdims=True) + RMS_EPS)).astype(x.dtype) * g


def rope(x, pos):
    half = QK_ROPE_DIM // 2
    inv = ROPE_THETA ** (-jnp.arange(half, dtype=jnp.float32) / half)
    ang = pos.astype(jnp.float32)[:, None] * inv[None, :]
    ang = ang.reshape(ang.shape[:1] + (1,) * (x.ndim - 3) + (half,))
    cos, sin = jnp.cos(ang), jnp.sin(ang)
    x1 = x[..., :half].astype(jnp.float32)
    x2 = x[..., half:].astype(jnp.float32)
    return jnp.concatenate([x1 * cos - x2 * sin, x1 * sin + x2 * cos], axis=-1).astype(x.dtype)


def mla_project(h, pos, w_dqkv, g_q, g_kv, w_uq, w_qr):
    z = h @ w_dqkv
    c_q = rms_norm(z[..., :Q_LORA_RANK], g_q)
    c_kv = rms_norm(z[..., Q_LORA_RANK:Q_LORA_RANK + KV_LORA_RANK], g_kv)
    k_rope = rope(z[..., Q_LORA_RANK + KV_LORA_RANK:], pos)
    q_nope = jnp.einsum('btq,qhd->bthd', c_q, w_uq)
    q_rope = rope(jnp.einsum('btq,qhr->bthr', c_q, w_qr), pos)
    return q_nope, q_rope, c_kv, k_rope


def mla_prompt(h, pos, w_dqkv, g_q, g_kv, w_uq, w_qr, w_uk, w_uv, w_o):
    b, s, _ = h.shape
    q_nope, q_rope, c_kv, k_rope = mla_project(h, pos, w_dqkv, g_q, g_kv, w_uq, w_qr)
    k_nope = jnp.einsum('bsc,chd->bshd', c_kv, w_uk)
    v = jnp.einsum('bsc,chd->bshd', c_kv, w_uv)
    n_blk = s // Q_BLOCK

    def block(args):
        qn, qr, qpos = args
        sc = (jnp.einsum('bqhd,bkhd->bhqk', qn, k_nope)
              + jnp.einsum('bqhr,bkr->bhqk', qr, k_rope)).astype(jnp.float32) * ATTN_SCALE
        sc = jnp.where(pos[None, :] <= qpos[:, None], sc, -jnp.inf)
        p = jax.nn.softmax(sc, axis=-1).astype(v.dtype)
        return jnp.einsum('bhqk,bkhd->bqhd', p, v)

    to_blocks = lambda a: jnp.moveaxis(a.reshape((b, n_blk, Q_BLOCK) + a.shape[2:]), 1, 0)
    o = lax.map(block, (to_blocks(q_nope), to_blocks(q_rope), pos.reshape(n_blk, Q_BLOCK)))
    o = jnp.moveaxis(o, 0, 1).reshape(b, s, N_HEADS * V_HEAD_DIM)
    return o @ w_o, c_kv, k_rope


def mla_sample(h, pos, cache_lat, cache_kr, page_table, j, w_dqkv, g_q, g_kv, w_uq, w_qr, w_uk, w_uv, w_o):
    b, t, _ = h.shape
    q_nope, q_rope, c_kv, k_rope = mla_project(h, pos, w_dqkv, g_q, g_kv, w_uq, w_qr)
    c_past = cache_lat[j, page_table].reshape(b, -1, KV_LORA_RANK)
    kr_past = cache_kr[j, page_table].reshape(b, -1, QK_ROPE_DIM)
    n_past = c_past.shape[1]
    q_lat = jnp.einsum('bthd,chd->bthc', q_nope, w_uk)

    def scores(c, kr):
        return (jnp.einsum('bthc,bkc->bhtk', q_lat, c)
                + jnp.einsum('bthr,bkr->bhtk', q_rope, kr)).astype(jnp.float32) * ATTN_SCALE

    s_new = jnp.where(jnp.tril(jnp.ones((t, t), dtype=bool)), scores(c_kv, k_rope), -jnp.inf)
    p = jax.nn.softmax(jnp.concatenate([scores(c_past, kr_past), s_new], axis=-1), axis=-1).astype(h.dtype)
    o_lat = (jnp.einsum('bhtk,bkc->bthc', p[..., :n_past], c_past)
             + jnp.einsum('bhtk,bkc->bthc', p[..., n_past:], c_kv))
    o = jnp.einsum('bthc,chd->bthd', o_lat, w_uv).reshape(b, t, N_HEADS * V_HEAD_DIM)
    return o @ w_o, c_kv, k_rope


def conv_module(h, buf, w_pw1, b_pw1, w_dw, b_dw, g_n, b_n, w_pw2, b_pw2):
    a, g = jnp.split(h @ w_pw1 + b_pw1, 2, axis=-1)
    u = a * jax.nn.sigmoid(g)
    ext = jnp.concatenate([buf.astype(u.dtype), u], axis=1)
    y = lax.conv_general_dilated(ext, w_dw[:, None, :], window_strides=(1,), padding='VALID',
                                 dimension_numbers=('NWC', 'WIO', 'NWC'),
                                 feature_group_count=D_CONV) + b_dw
    y = jax.nn.silu(layer_norm(y, g_n, b_n))
    return y @ w_pw2 + b_pw2, ext[:, -(CONV_WIDTH - 1):]


def sgu_in(h, w_in, b_in, g_v, b_v):
    u, v = jnp.split(jax.nn.gelu(h @ w_in + b_in), 2, axis=-1)
    return u, layer_norm(v, g_v, b_v)


def sgu_mix(u, v, w_s, b_s, w_out, b_out):
    n_rows = v.shape[2]
    w = jnp.where(jnp.tril(jnp.ones((n_rows, n_rows), dtype=bool)), w_s[:, :n_rows, :n_rows], 0.0)
    vg = v.reshape(v.shape[:3] + (N_SGU_GROUPS, SGU_GROUP_DIM))
    gate = jnp.einsum('gij,bnjgd->bnigd', w, vg) + b_s[:, :n_rows].T[:, :, None]
    return (u * gate.reshape(u.shape)) @ w_out + b_out


def sgu_prompt(h, w_in, b_in, g_v, b_v, w_s, b_s, w_out, b_out):
    b, s, _ = h.shape
    u, v = sgu_in(h, w_in, b_in, g_v, b_v)
    shp = (b, s // CHUNK, CHUNK, D_SGU)
    y = sgu_mix(u.reshape(shp), v.reshape(shp), w_s, b_s, w_out, b_out)
    return y.reshape(b, s, D_MODEL), v[:, -CHUNK:]


def sgu_sample(h, w_in, b_in, g_v, b_v, w_s, b_s, w_out, b_out):
    b, t, _ = h.shape
    u, v = sgu_in(h, w_in, b_in, g_v, b_v)
    y = sgu_mix(u[:, None], v[:, None], w_s, b_s, w_out, b_out)
    return y.reshape(b, t, D_MODEL), v


def swiglu(h, w_gu, w_down):
    g, u = jnp.split(h @ w_gu, 2, axis=-1)
    return (jax.nn.silu(g) * u) @ w_down


def moe(h, w_r, b_r, w_gu, w_down):
    logits = (h @ w_r).astype(jnp.float32) + b_r
    top_v, top_i = lax.top_k(logits, TOP_K)
    wts = jax.nn.softmax(top_v, axis=-1)
    gates = jnp.sum(jax.nn.one_hot(top_i, N_EXPERTS, dtype=jnp.float32) * wts[..., None], axis=-2).astype(h.dtype)
    y = jnp.zeros_like(h)
    for e in range(N_EXPERTS):
        y = y + gates[..., e:e + 1] * swiglu(h, w_gu[e], w_down[e])
    return y


def setup_inputs(seed: int = 0) -> dict:
    key = jax.random.key(seed)
    ks = iter(jax.random.split(key, 64))
    nrm = lambda shape, scale: jax.random.normal(next(ks), shape, jnp.float32) * scale
    gain = lambda shape, s=0.01: 1.0 + jax.random.normal(next(ks), shape, jnp.float32) * s
    n_pages = PAST_LEN // PAGE_SIZE
    n_used = DEC_BATCH * n_pages
    n_pool = n_used + max(1, n_used // 4)
    d = D_MODEL
    x_prompt = nrm((BATCH, SEQ, d), 1.0)
    x_sample = nrm((DEC_BATCH, DEC_SEQ, d), 1.0)
    cache_latent = nrm((N_A_LAYERS, n_pool, PAGE_SIZE, KV_LORA_RANK), 1.0)
    cache_krope = nrm((N_A_LAYERS, n_pool, PAGE_SIZE, QK_ROPE_DIM), 1.0)
    state_conv = nrm((N_B_LAYERS, DEC_BATCH, CONV_WIDTH - 1, D_CONV), 0.5)
    perm = jax.random.permutation(next(ks), n_pool)
    page_table = perm[:n_used].reshape(DEC_BATCH, n_pages).astype(jnp.int32)
    return {
        'x_prompt': x_prompt, 'x_sample': x_sample,
        'cache_latent': cache_latent, 'cache_krope': cache_krope, 'state_conv': state_conv,
        'page_table': page_table,
        'c_prompt': nrm((BATCH, d), 1.0), 'c_sample': nrm((DEC_BATCH, d), 1.0),
        'w_ada': nrm((DEPTH, d, N_MOD * d), 0.1 * d ** -0.5),
        'b_ada': nrm((DEPTH, N_MOD * d), 0.01),
        'ln_g': gain((DEPTH, 2, d)), 'ln_b': nrm((DEPTH, 2, d), 0.01),
        'mla_w_dqkv': nrm((N_A_LAYERS, d, Q_LORA_RANK + KV_LORA_RANK + QK_ROPE_DIM), d ** -0.5),
        'mla_g_q': gain((N_A_LAYERS, Q_LORA_RANK)), 'mla_g_kv': gain((N_A_LAYERS, KV_LORA_RANK)),
        'mla_w_uq': nrm((N_A_LAYERS, Q_LORA_RANK, N_HEADS, QK_NOPE_DIM), Q_LORA_RANK ** -0.5),
        'mla_w_qr': nrm((N_A_LAYERS, Q_LORA_RANK, N_HEADS, QK_ROPE_DIM), Q_LORA_RANK ** -0.5),
        'mla_w_uk': nrm((N_A_LAYERS, KV_LORA_RANK, N_HEADS, QK_NOPE_DIM), KV_LORA_RANK ** -0.5),
        'mla_w_uv': nrm((N_A_LAYERS, KV_LORA_RANK, N_HEADS, V_HEAD_DIM), BETA * KV_LORA_RANK ** -0.5),
        'mla_w_o': nrm((N_A_LAYERS, N_HEADS * V_HEAD_DIM, d), BETA * (N_HEADS * V_HEAD_DIM) ** -0.5),
        'conv_w_pw1': nrm((N_B_LAYERS, d, 2 * D_CONV), d ** -0.5),
        'conv_b_pw1': nrm((N_B_LAYERS, 2 * D_CONV), 0.01),
        'conv_w_dw': nrm((N_B_LAYERS, CONV_WIDTH, D_CONV), CONV_WIDTH ** -0.5),
        'conv_b_dw': nrm((N_B_LAYERS, D_CONV), 0.01),
        'conv_g_n': gain((N_B_LAYERS, D_CONV)), 'conv_b_n': nrm((N_B_LAYERS, D_CONV), 0.01),
        'conv_w_pw2': nrm((N_B_LAYERS, D_CONV, d), BETA * D_CONV ** -0.5),
        'conv_b_pw2': nrm((N_B_LAYERS, d), 0.01),
        'sgu_w_in': nrm((N_C_LAYERS, d, 2 * D_SGU), d ** -0.5),
        'sgu_b_in': nrm((N_C_LAYERS, 2 * D_SGU), 0.01),
        'sgu_g_v': gain((N_C_LAYERS, D_SGU)), 'sgu_b_v': nrm((N_C_LAYERS, D_SGU), 0.01),
        'sgu_w_s': nrm((N_C_LAYERS, N_SGU_GROUPS, CHUNK, CHUNK), CHUNK ** -0.5),
        'sgu_b_s': gain((N_C_LAYERS, N_SGU_GROUPS, CHUNK), 0.1),
        'sgu_w_out': nrm((N_C_LAYERS, D_SGU, d), BETA * D_SGU ** -0.5),
        'sgu_b_out': nrm((N_C_LAYERS, d), 0.01),
        'ffn_w_gu': nrm((N_DENSE_LAYERS, d, 2 * DENSE_FF), d ** -0.5),
        'ffn_w_down': nrm((N_DENSE_LAYERS, DENSE_FF, d), BETA * DENSE_FF ** -0.5),
        'moe_w_r': nrm((N_MOE_LAYERS, d, N_EXPERTS), d ** -0.5),
        'moe_b_r': nrm((N_MOE_LAYERS, N_EXPERTS), 0.01),
        'moe_w_gu': nrm((N_MOE_LAYERS, N_EXPERTS, d, 2 * EXPERT_FF), d ** -0.5),
        'moe_w_down': nrm((N_MOE_LAYERS, N_EXPERTS, EXPERT_FF, d), BETA * EXPERT_FF ** -0.5),
    }


def reference(x_prompt, x_sample, cache_latent, cache_krope, state_conv, page_table, c_prompt, c_sample,
              w_ada, b_ada, ln_g, ln_b,
              mla_w_dqkv, mla_g_q, mla_g_kv, mla_w_uq, mla_w_qr, mla_w_uk, mla_w_uv, mla_w_o,
              conv_w_pw1, conv_b_pw1, conv_w_dw, conv_b_dw, conv_g_n, conv_b_n, conv_w_pw2, conv_b_pw2,
              sgu_w_in, sgu_b_in, sgu_g_v, sgu_b_v, sgu_w_s, sgu_b_s, sgu_w_out, sgu_b_out,
              ffn_w_gu, ffn_w_down, moe_w_r, moe_b_r, moe_w_gu, moe_w_down):
    seq = x_prompt.shape[1]
    t_new = x_sample.shape[1]
    past = page_table.shape[1] * cache_latent.shape[2]
    pos_p = jnp.arange(seq, dtype=jnp.int32)
    pos_s = past + jnp.arange(t_new, dtype=jnp.int32)

    mla_p = lambda j: (mla_w_dqkv[j], mla_g_q[j], mla_g_kv[j], mla_w_uq[j], mla_w_qr[j],
                       mla_w_uk[j], mla_w_uv[j], mla_w_o[j])
    conv_p = lambda j: (conv_w_pw1[j], conv_b_pw1[j], conv_w_dw[j], conv_b_dw[j],
                        conv_g_n[j], conv_b_n[j], conv_w_pw2[j], conv_b_pw2[j])
    sgu_p = lambda j: (sgu_w_in[j], sgu_b_in[j], sgu_g_v[j], sgu_b_v[j],
                       sgu_w_s[j], sgu_b_s[j], sgu_w_out[j], sgu_b_out[j])

    def run(x, c, attn_fn, conv_fn, sgu_fn):
        new_lat, new_kr, new_conv, new_v = [], [], [], []
        for l in range(DEPTH):
            mod = (jax.nn.silu(c) @ w_ada[l] + b_ada[l])[:, None, :]
            sh_m, sc_m, gt_m, sh_f, sc_f, gt_f = jnp.split(mod, N_MOD, axis=-1)
            h = x * (1 + sc_m) + sh_m
            kind, j = l % N_MIXERS, l // N_MIXERS
            if kind == 0:
                m, lat, kr = attn_fn(j, h)
                new_lat.append(lat)
                new_kr.append(kr)
            elif kind == 1:
                m, buf = conv_fn(j, h)
                new_conv.append(buf)
            else:
                m, v = sgu_fn(j, h)
                new_v.append(v)
            x = layer_norm(ALPHA * x + (1 + gt_m) * m, ln_g[l, 0], ln_b[l, 0])
            h = x * (1 + sc_f) + sh_f
            if l % 2 == 0:
                f = swiglu(h, ffn_w_gu[l // 2], ffn_w_down[l // 2])
            else:
                f = moe(h, moe_w_r[l // 2], moe_b_r[l // 2], moe_w_gu[l // 2], moe_w_down[l // 2])
            x = layer_norm(ALPHA * x + (1 + gt_f) * f, ln_g[l, 1], ln_b[l, 1])
        return x, jnp.stack(new_lat), jnp.stack(new_kr), jnp.stack(new_conv), jnp.stack(new_v)

    y_prompt, lat_p, kr_p, conv_st_p, v_p = run(
        x_prompt, c_prompt,
        lambda j, h: mla_prompt(h, pos_p, *mla_p(j)),
        lambda j, h: conv_module(h, jnp.zeros((h.shape[0], CONV_WIDTH - 1, D_CONV), h.dtype), *conv_p(j)),
        lambda j, h: sgu_prompt(h, *sgu_p(j)))
    y_sample, lat_s, kr_s, conv_st_s, v_s = run(
        x_sample, c_sample,
        lambda j, h: mla_sample(h, pos_s, cache_latent, cache_krope, page_table, j, *mla_p(j)),
        lambda j, h: conv_module(h, state_conv[j], *conv_p(j)),
        lambda j, h: sgu_sample(h, *sgu_p(j)))
    return (y_prompt, y_sample, lat_p, kr_p, conv_st_p, v_p, lat_s, kr_s, conv_st_s, v_s)
```

```python
import functools

import jax
import jax.numpy as jnp
from jax import lax
from jax.experimental import pallas as pl
from jax.experimental.pallas import tpu as pltpu

F32 = jnp.float32
BF16 = jnp.bfloat16

D_MODEL = 2048
DEPTH = 4
PAGE_SIZE = 128
N_HEADS = 16
QK_NOPE_DIM = 128
QK_ROPE_DIM = 64
V_HEAD_DIM = 128
Q_LORA_RANK = 512
KV_LORA_RANK = 512
ROPE_THETA = 10000.0
ATTN_SCALE = (QK_NOPE_DIM + QK_ROPE_DIM) ** -0.5
CONV_WIDTH = 31
CHUNK = 128
D_SGU = 2 * D_MODEL
N_SGU_GROUPS = 16
SGU_GROUP_DIM = D_SGU // N_SGU_GROUPS
DENSE_FF = 5632
N_EXPERTS = 8
TOP_K = 2
EXPERT_FF = 7168
N_MOD = 6
ALPHA = (2 * DEPTH) ** 0.25
LN_EPS = 1e-5
RMS_EPS = 1e-6

LANES = 128
V7X_VMEM_LIMIT = 56 * 1024 * 1024
CONV_HALO = 32
PAGES_PER_STEP = 8


def _params(sem):
    return pltpu.CompilerParams(dimension_semantics=sem, vmem_limit_bytes=V7X_VMEM_LIMIT)


def _ln_rows(v, g, b):
    mu = jnp.mean(v, axis=-1, keepdims=True)
    vc = v - mu
    var = jnp.mean(vc * vc, axis=-1, keepdims=True)
    return vc * lax.rsqrt(var + LN_EPS) * g + b


def _gelu_tanh(y):
    return 0.5 * y * (1.0 + jnp.tanh(0.7978845608028654 * (y + 0.044715 * (y * y * y))))


def _proj(x, ws, epi, outs, *, tm, tn, nj, bs=None, mod=None, extras=(), silu_in=False,
          gather=None, expert=None, name):
    if bs is None:
        bs = [None] * len(ws)
    k_dim = x.shape[1]
    m_rows = gather.shape[0] if gather is not None else x.shape[0]
    ni = m_rows // tm
    assert ni * tm == m_rows
    has_mod = mod is not None
    prefetch = []
    if gather is not None:
        prefetch.append(gather)
    if expert is not None:
        prefetch.extend(expert)
    n_pf = len(prefetch)
    direct = (x.dtype == BF16 and not has_mod and not silu_in and gather is None)

    def w_imap(lead, off):
        def f(i, j, *pf):
            jj = j
            lead_idx = []
            if expert is not None:
                te, nu = pf[n_pf - 2], pf[n_pf - 1]
                jj = jnp.where(i < nu[0], j, nj - 1)
            for s in lead:
                lead_idx.append(pf[n_pf - 2][i] if s == 'e' else s)
            return tuple(lead_idx) + (0, off + jj)
        return f

    def b_imap(lead, off):
        return lambda i, j, *pf: (lead, 0, off + j)

    in_specs, args = [], []
    if gather is not None:
        in_specs.append(pl.BlockSpec(memory_space=pl.ANY))
    else:
        in_specs.append(pl.BlockSpec((tm, k_dim), lambda i, j, *pf: (i, 0)))
    args.append(x)
    if has_mod:
        marr, c_sh, c_sc, rows_per_b = mod
        r = marr.shape[1]
        for comp in (c_sh, c_sc):
            in_specs.append(pl.BlockSpec(
                (None, r, k_dim), lambda i, j, *pf, comp=comp: ((i * tm) // rows_per_b, 0, comp)))
            args.append(marr)
    for arr, lead, off in ws:
        in_specs.append(pl.BlockSpec((None,) * len(lead) + (k_dim, tn), w_imap(lead, off)))
        args.append(arr)
    for b in bs:
        if b is not None:
            arr, lead, off = b
            in_specs.append(pl.BlockSpec((None, 1, tn), b_imap(lead, off)))
            args.append(arr)
    for arr, bshape, imap in extras:
        in_specs.append(pl.BlockSpec(bshape, lambda i, j, *pf, imap=imap: imap(i, j)))
        args.append(arr)
    out_specs = [pl.BlockSpec(bshape, lambda i, j, *pf, imap=imap: imap(i, j))
                 for (_, _, bshape, imap) in outs]
    out_shape = [jax.ShapeDtypeStruct(shape, dtype) for (shape, dtype, _, _) in outs]

    scratch = []
    if not direct:
        scratch.append(pltpu.VMEM((tm, k_dim), BF16))
    if gather is not None:
        scratch.append(pltpu.VMEM((tm, k_dim), x.dtype))
        scratch.append(pltpu.SemaphoreType.DMA(()))

    def body(*refs):
        it = iter(refs)
        pf = [next(it) for _ in range(n_pf)]
        x_ref = next(it)
        sh_ref = sc_ref = None
        if has_mod:
            sh_ref, sc_ref = next(it), next(it)
        w_refs = [next(it) for _ in ws]
        b_refs = [next(it) if b is not None else None for b in bs]
        e_refs = [next(it) for _ in extras]
        o_refs = [next(it) for _ in outs]
        h_ref = None if direct else next(it)
        i, j = pl.program_id(0), pl.program_id(1)

        if gather is not None:
            xg_ref, sem = next(it), next(it)
            rs_ref = pf[0]

            @pl.when(j == 0)
            def _():
                base = i * tm

                def issue(r, c):
                    src = rs_ref[base + r]
                    pltpu.make_async_copy(x_ref.at[pl.ds(src, 1)], xg_ref.at[pl.ds(r, 1)], sem).start()
                    return c

                lax.fori_loop(0, tm, issue, 0)

                def wait(r, c):
                    pltpu.make_async_copy(x_ref.at[pl.ds(0, 1)], xg_ref.at[pl.ds(r, 1)], sem).wait()
                    return c

                lax.fori_loop(0, tm, wait, 0)
                h_ref[...] = xg_ref[...].astype(BF16)
        elif not direct:
            @pl.when(j == 0)
            def _():
                h = x_ref[...].astype(F32)
                if has_mod:
                    h = h * (1.0 + sc_ref[...]) + sh_ref[...]
                if silu_in:
                    h = h * jax.nn.sigmoid(h)
                h_ref[...] = h.astype(BF16)

        def compute():
            h = x_ref[...] if direct else h_ref[...]
            ys = []
            for w_ref, b_ref in zip(w_refs, b_refs):
                y = jnp.dot(h, w_ref[...].astype(BF16), preferred_element_type=F32)
                if b_ref is not None:
                    y = y + b_ref[...]
                ys.append(y)
            res = epi(ys, [e[...] for e in e_refs])
            for o_ref, r_ in zip(o_refs, res):
                o_ref[...] = r_.astype(o_ref.dtype)

        if expert is not None:
            nu_ref = pf[n_pf - 1]

            @pl.when(i < nu_ref[0])
            def _():
                compute()

            @pl.when(i >= nu_ref[0])
            def _():
                for o_ref in o_refs:
                    o_ref[...] = jnp.zeros(o_ref.shape, o_ref.dtype)
        else:
            compute()

    grid_spec = pltpu.PrefetchScalarGridSpec(
        num_scalar_prefetch=n_pf, grid=(ni, nj), in_specs=in_specs, out_specs=out_specs,
        scratch_shapes=scratch)
    res = pl.pallas_call(body, grid_spec=grid_spec, out_shape=out_shape,
                         compiler_params=_params(("parallel", "arbitrary")), name=name)(*prefetch, *args)
    return res


def _proj_k(a, w, lead, *, tm, tk, res=None, bias=None, expert=None, name):
    m_rows, k_dim = a.shape
    n_dim = w.shape[-1]
    ni, nk = m_rows // tm, k_dim // tk
    assert ni * tm == m_rows and nk * tk == k_dim
    prefetch = list(expert) if expert is not None else []
    n_pf = len(prefetch)

    def a_imap(i, k, *pf):
        kk = k
        if expert is not None:
            kk = jnp.where(i < pf[1][0], k, nk - 1)
        return (i, kk)

    def w_imap(i, k, *pf):
        kk = k
        if expert is not None:
            kk = jnp.where(i < pf[1][0], k, nk - 1)
        return tuple(pf[0][i] if s == 'e' else s for s in lead) + (kk, 0)

    in_specs = [pl.BlockSpec((tm, tk), a_imap),
                pl.BlockSpec((None,) * len(lead) + (tk, n_dim), w_imap)]
    args = [a, w]
    if bias is not None:
        barr, bl = bias
        in_specs.append(pl.BlockSpec((None, 1, n_dim), lambda i, k, *pf: (bl, 0, 0)))
        args.append(barr)
    if res is not None:
        x_res, marr, c_gate, rows_per_b, ln_g, ln_b, ln_idx = res
        r = marr.shape[1]
        in_specs += [
            pl.BlockSpec((tm, n_dim), lambda i, k, *pf: (i, 0)),
            pl.BlockSpec((None, r, n_dim), lambda i, k, *pf: ((i * tm) // rows_per_b, 0, c_gate)),
            pl.BlockSpec((None, 1, n_dim), lambda i, k, *pf: (ln_idx, 0, 0)),
            pl.BlockSpec((None, 1, n_dim), lambda i, k, *pf: (ln_idx, 0, 0)),
        ]
        args += [x_res, marr, ln_g, ln_b]

    def body(*refs):
        it = iter(refs)
        pf = [next(it) for _ in range(n_pf)]
        a_ref, w_ref = next(it), next(it)
        b_ref = next(it) if bias is not None else None
        if res is not None:
            x_ref, gt_ref, g_ref, be_ref = next(it), next(it), next(it), next(it)
        o_ref, acc_ref = next(it), next(it)
        i, k = pl.program_id(0), pl.program_id(1)

        def compute():
            @pl.when(k == 0)
            def _():
                acc_ref[...] = jnp.zeros(acc_ref.shape, F32)

            acc_ref[...] += jnp.dot(a_ref[...].astype(BF16), w_ref[...].astype(BF16),
                                    preferred_element_type=F32)

            @pl.when(k == nk - 1)
            def _():
                y = acc_ref[...]
                if b_ref is not None:
                    y = y + b_ref[...]
                if res is not None:
                    y = _ln_rows(ALPHA * x_ref[...] + (1.0 + gt_ref[...]) * y, g_ref[...], be_ref[...])
                o_ref[...] = y.astype(o_ref.dtype)

        if expert is not None:
            @pl.when(i < pf[1][0])
            def _():
                compute()

            @pl.when(jnp.logical_and(i >= pf[1][0], k == nk - 1))
            def _():
                o_ref[...] = jnp.zeros(o_ref.shape, o_ref.dtype)
        else:
            compute()

    grid_spec = pltpu.PrefetchScalarGridSpec(
        num_scalar_prefetch=n_pf, grid=(ni, nk), in_specs=in_specs,
        out_specs=pl.BlockSpec((tm, n_dim), lambda i, k, *pf: (i, 0)),
        scratch_shapes=[pltpu.VMEM((tm, n_dim), F32)])
    return pl.pallas_call(body, grid_spec=grid_spec,
                          out_shape=jax.ShapeDtypeStruct((m_rows, n_dim), F32),
                          compiler_params=_params(("parallel", "arbitrary")), name=name)(*prefetch, *args)


def _mla_down(x, mod, w_dqkv, l, w_ka, w_kb, g_q, g_kv, tab_a, tab_b, *, tm, name):
    m_rows = x.shape[0]
    ni = m_rows // tm
    marr, rows_per_b = mod
    r = marr.shape[1]
    n_tab = tab_a.shape[0] // tm

    def body(x_ref, sh_ref, sc_ref, wq_ref, wkv_ref, wka_ref, wkb_ref, gq_ref, gkv_ref, ta_ref, tb_ref,
             cq_ref, ckv_ref, kr_ref):
        h = (x_ref[...] * (1.0 + sc_ref[...]) + sh_ref[...]).astype(BF16)

        def rms(w_ref, g_ref):
            z = jnp.dot(h, w_ref[...].astype(BF16), preferred_element_type=F32)
            return z * lax.rsqrt(jnp.mean(z * z, axis=-1, keepdims=True) + RMS_EPS) * g_ref[...]

        cq_ref[...] = rms(wq_ref, gq_ref).astype(cq_ref.dtype)
        ckv_ref[...] = rms(wkv_ref, gkv_ref)
        ka = jnp.dot(h, wka_ref[...].astype(BF16), preferred_element_type=F32)
        kb = jnp.dot(h, wkb_ref[...].astype(BF16), preferred_element_type=F32)
        kr_ref[...] = ka * ta_ref[...] + kb * tb_ref[...]

    rank = Q_LORA_RANK
    in_specs = [
        pl.BlockSpec((tm, D_MODEL), lambda i: (i, 0)),
        pl.BlockSpec((None, r, D_MODEL), lambda i: ((i * tm) // rows_per_b, 0, 0)),
        pl.BlockSpec((None, r, D_MODEL), lambda i: ((i * tm) // rows_per_b, 0, 1)),
        pl.BlockSpec((None, D_MODEL, rank), lambda i: (l, 0, 0)),
        pl.BlockSpec((None, D_MODEL, rank), lambda i: (l, 0, 1)),
        pl.BlockSpec((D_MODEL, LANES), lambda i: (0, 0)),
        pl.BlockSpec((D_MODEL, LANES), lambda i: (0, 0)),
        pl.BlockSpec((None, 1, rank), lambda i: (l, 0, 0)),
        pl.BlockSpec((None, 1, rank), lambda i: (l, 0, 0)),
        pl.BlockSpec((tm, LANES), lambda i: (i % n_tab, 0)),
        pl.BlockSpec((tm, LANES), lambda i: (i % n_tab, 0)),
    ]
    out_specs = [pl.BlockSpec((tm, rank), lambda i: (i, 0)),
                 pl.BlockSpec((tm, rank), lambda i: (i, 0)),
                 pl.BlockSpec((tm, LANES), lambda i: (i, 0))]
    out_shape = [jax.ShapeDtypeStruct((m_rows, rank), BF16),
                 jax.ShapeDtypeStruct((m_rows, rank), F32),
                 jax.ShapeDtypeStruct((m_rows, LANES), F32)]
    return pl.pallas_call(body, grid=(ni,), in_specs=in_specs, out_specs=out_specs, out_shape=out_shape,
                          compiler_params=_params(("parallel",)), name=name)(
        x, marr, marr, w_dqkv, w_dqkv, w_ka, w_kb, g_q, g_kv, tab_a, tab_b)


def _attn_prompt(qn, qr, kn, kr, v, *, n_batch, seq, tq, name):
    hd = QK_NOPE_DIM
    n_q = seq // tq

    def body(qn_ref, qr_ref, kn_ref, kr_ref, v_ref, o_ref):
        k_all = jnp.concatenate([kn_ref[...], kr_ref[...].astype(BF16)], axis=1)
        for qi in range(n_q):
            rows = slice(qi * tq, (qi + 1) * tq)
            n_kv = (qi + 1) * tq
            q = jnp.concatenate([qn_ref[rows, :], qr_ref[rows, :]], axis=1)
            s = lax.dot_general(q, k_all[:n_kv], (((1,), (1,)), ((), ())),
                                preferred_element_type=F32) * ATTN_SCALE
            row = lax.broadcasted_iota(jnp.int32, (tq, n_kv), 0) + qi * tq
            col = lax.broadcasted_iota(jnp.int32, (tq, n_kv), 1)
            s = jnp.where(col <= row, s, -jnp.inf)
            m = jnp.max(s, axis=-1, keepdims=True)
            p = jnp.exp(s - m)
            den = jnp.sum(p, axis=-1, keepdims=True)
            o = jnp.dot(p.astype(BF16), v_ref[0:n_kv, :], preferred_element_type=F32) / den
            o_ref[rows, :] = o.astype(o_ref.dtype)

    hspec = pl.BlockSpec((seq, hd), lambda b, h: (b, h))
    in_specs = [hspec, hspec, hspec, pl.BlockSpec((seq, LANES), lambda b, h: (b, 0)), hspec]
    return pl.pallas_call(body, grid=(n_batch, N_HEADS), in_specs=in_specs, out_specs=hspec,
                          out_shape=jax.ShapeDtypeStruct(qn.shape, BF16),
                          compiler_params=_params(("parallel", "parallel")), name=name)(qn, qr, kn, kr, v)


def _head_mix(a, w, l, *, contract_w_last, name):
    n_b = a.shape[0]
    c_dim = w.shape[1]
    hd = QK_NOPE_DIM

    def body(a_ref, w_ref, o_ref):
        wv = w_ref[...].astype(BF16)
        av = a_ref[...].astype(BF16)
        if contract_w_last:
            o_ref[...] = lax.dot_general(av, wv, (((1,), (1,)), ((), ())), preferred_element_type=F32)
        else:
            o_ref[...] = jnp.dot(av, wv, preferred_element_type=F32)

    a_w, o_w = (hd, c_dim) if contract_w_last else (c_dim, hd)
    return pl.pallas_call(
        body, grid=(N_HEADS,),
        in_specs=[pl.BlockSpec((n_b, a_w), lambda h: (0, h)),
                  pl.BlockSpec((None, c_dim, hd), lambda h: (l, 0, h))],
        out_specs=pl.BlockSpec((n_b, o_w), lambda h: (0, h)),
        out_shape=jax.ShapeDtypeStruct((n_b, N_HEADS * o_w), F32),
        compiler_params=_params(("parallel",)), name=name)(a, w)


def _attn_decode(q_lat, q_rope, c_new, kr_new, cache_lat, cache_kr, page_table, l, *, name):
    n_b, n_pages = page_table.shape
    g = PAGES_PER_STEP
    n_steps = n_pages // g
    assert n_steps * g == n_pages
    c_dim, r_dim = KV_LORA_RANK, QK_ROPE_DIM
    pt_flat = page_table.reshape(-1)

    def body(pt_ref, ql_ref, qr_ref, cn_ref, kn_ref, *rest):
        c_refs = rest[:g]
        kr_refs = rest[g:2 * g]
        o_ref, m_ref, d_ref, acc_ref = rest[2 * g:]
        s_idx = pl.program_id(1)
        ql = ql_ref[...]
        qr = qr_ref[...][:, :r_dim]

        @pl.when(s_idx == 0)
        def _():
            s_new = (jnp.sum(ql * cn_ref[...], axis=-1, keepdims=True)
                     + jnp.sum(qr * kn_ref[...][:, :r_dim], axis=-1, keepdims=True)) * ATTN_SCALE
            m_ref[...] = s_new
            d_ref[...] = jnp.ones(d_ref.shape, F32)
            acc_ref[...] = jnp.broadcast_to(cn_ref[...], acc_ref.shape)

        qlb, qrb = ql.astype(BF16), qr.astype(BF16)
        cs = [c_ref[...].astype(BF16) for c_ref in c_refs]
        nt = (((1,), (1,)), ((), ()))
        s = jnp.concatenate(
            [lax.dot_general(qlb, cs[t], nt, preferred_element_type=F32)
             + lax.dot_general(qrb, kr_refs[t][...].astype(BF16), nt, preferred_element_type=F32)
             for t in range(g)], axis=1) * ATTN_SCALE
        m_old = m_ref[...]
        m_new = jnp.maximum(m_old, jnp.max(s, axis=-1, keepdims=True))
        scale_old = jnp.exp(m_old - m_new)
        p = jnp.exp(s - m_new)
        d_ref[...] = d_ref[...] * scale_old + jnp.sum(p, axis=-1, keepdims=True)
        pb = p.astype(BF16)
        pv = jnp.dot(pb[:, 0:PAGE_SIZE], cs[0], preferred_element_type=F32)
        for t in range(1, g):
            pv = pv + jnp.dot(pb[:, t * PAGE_SIZE:(t + 1) * PAGE_SIZE], cs[t], preferred_element_type=F32)
        acc_ref[...] = acc_ref[...] * scale_old + pv
        m_ref[...] = m_new

        @pl.when(s_idx == n_steps - 1)
        def _():
            o_ref[...] = acc_ref[...] / d_ref[...]

    def page_imap(t):
        return lambda b, s, pt: (l, pt[b * n_pages + s * g + t], 0, 0)

    in_specs = [
        pl.BlockSpec((None, N_HEADS, c_dim), lambda b, s, pt: (b, 0, 0)),
        pl.BlockSpec((None, N_HEADS, LANES), lambda b, s, pt: (b, 0, 0)),
        pl.BlockSpec((None, 1, c_dim), lambda b, s, pt: (b, 0, 0)),
        pl.BlockSpec((None, 1, LANES), lambda b, s, pt: (b, 0, 0)),
    ]
    in_specs += [pl.BlockSpec((None, None, PAGE_SIZE, c_dim), page_imap(t)) for t in range(g)]
    in_specs += [pl.BlockSpec((None, None, PAGE_SIZE, r_dim), page_imap(t)) for t in range(g)]
    grid_spec = pltpu.PrefetchScalarGridSpec(
        num_scalar_prefetch=1, grid=(n_b, n_steps), in_specs=in_specs,
        out_specs=pl.BlockSpec((None, N_HEADS, c_dim), lambda b, s, pt: (b, 0, 0)),
        scratch_shapes=[pltpu.VMEM((N_HEADS, 1), F32), pltpu.VMEM((N_HEADS, 1), F32),
                        pltpu.VMEM((N_HEADS, c_dim), F32)])
    return pl.pallas_call(body, grid_spec=grid_spec,
                          out_shape=jax.ShapeDtypeStruct((n_b, N_HEADS, c_dim), F32),
                          compiler_params=_params(("parallel", "arbitrary")), name=name)(
        pt_flat, q_lat, q_rope, c_new, kr_new, *([cache_lat] * g), *([cache_kr] * g))


def _dwconv_prompt(u, w_dw, b_dw, g_n, b_n, *, n_batch, seq, tm, name):
    n_ch = u.shape[1]
    n_cc = n_ch // LANES
    nt = seq // tm
    halo_blocks = tm // CONV_HALO
    rc = 64
    first = CONV_HALO - (CONV_WIDTH - 1)

    def body(u_ref, halo_ref, w_ref, bd_ref, g_ref, b_ref, o_ref, ext_ref, y_ref):
        t, c = pl.program_id(1), pl.program_id(2)
        ext_ref[0:CONV_HALO, :] = jnp.where(t == 0, 0.0, halo_ref[...])
        ext_ref[CONV_HALO:, :] = u_ref[...]
        taps = [w_ref[k:k + 1, :] for k in range(CONV_WIDTH)]
        for r0 in range(0, tm, rc):
            acc = jnp.zeros((rc, LANES), F32)
            for k in range(CONV_WIDTH):
                acc = acc + taps[k] * ext_ref[r0 + first + k:r0 + first + k + rc, :]
            y_ref[c, r0:r0 + rc, :] = acc + bd_ref[...]

        @pl.when(c == n_cc - 1)
        def _():
            y = y_ref[...]
            mu = jnp.sum(jnp.sum(y, axis=0), axis=-1, keepdims=True) / n_ch
            yc = y - mu[None]
            var = jnp.sum(jnp.sum(yc * yc, axis=0), axis=-1, keepdims=True) / n_ch
            inv = lax.rsqrt(var + LN_EPS)
            for cc in range(n_cc):
                cols = slice(cc * LANES, (cc + 1) * LANES)
                z = yc[cc] * inv * g_ref[:, cols] + b_ref[:, cols]
                o_ref[:, cols] = (z * jax.nn.sigmoid(z)).astype(o_ref.dtype)

    def halo_imap(b, t, c):
        return (jnp.maximum((b * seq + t * tm) // CONV_HALO - 1, 0), c)

    in_specs = [
        pl.BlockSpec((tm, LANES), lambda b, t, c: (b * nt + t, c)),
        pl.BlockSpec((CONV_HALO, LANES), halo_imap),
        pl.BlockSpec((CONV_WIDTH, LANES), lambda b, t, c: (0, c)),
        pl.BlockSpec((1, LANES), lambda b, t, c: (0, c)),
        pl.BlockSpec((1, n_ch), lambda b, t, c: (0, 0)),
        pl.BlockSpec((1, n_ch), lambda b, t, c: (0, 0)),
    ]
    del halo_blocks
    return pl.pallas_call(
        body, grid=(n_batch, nt, n_cc), in_specs=in_specs,
        out_specs=pl.BlockSpec((tm, n_ch), lambda b, t, c: (b * nt + t, 0)),
        out_shape=jax.ShapeDtypeStruct(u.shape, BF16),
        scratch_shapes=[pltpu.VMEM((tm + CONV_HALO, LANES), F32), pltpu.VMEM((n_cc, tm, LANES), F32)],
        compiler_params=_params(("parallel", "parallel", "arbitrary")), name=name)(
        u, u, w_dw, b_dw, g_n, b_n)


def _dwconv_decode(u, buf, w_dw, b_dw, g_n, b_n, *, name):
    n_b, n_ch = u.shape
    bt = 8

    def body(u_ref, buf_ref, w_ref, bd_ref, g_ref, b_ref, o_ref):
        w = w_ref[...]
        y = jnp.sum(buf_ref[...] * w[None, :CONV_WIDTH - 1, :], axis=1)
        y = y + u_ref[...] * w[CONV_WIDTH - 1:CONV_WIDTH, :] + bd_ref[...]
        z = _ln_rows(y, g_ref[...], b_ref[...])
        o_ref[...] = (z * jax.nn.sigmoid(z)).astype(o_ref.dtype)

    vec = pl.BlockSpec((1, n_ch), lambda i: (0, 0))
    return pl.pallas_call(
        body, grid=(n_b // bt,),
        in_specs=[pl.BlockSpec((bt, n_ch), lambda i: (i, 0)),
                  pl.BlockSpec((bt, CONV_WIDTH - 1, n_ch), lambda i: (i, 0, 0)),
                  pl.BlockSpec((CONV_WIDTH, n_ch), lambda i: (0, 0)), vec, vec, vec],
        out_specs=pl.BlockSpec((bt, n_ch), lambda i: (i, 0)),
        out_shape=jax.ShapeDtypeStruct((n_b, n_ch), BF16),
        compiler_params=_params(("parallel",)), name=name)(u, buf, w_dw, b_dw, g_n, b_n)


def _sgu_gate_prompt(uv, w_s, b_s_t, g_v, b_v, *, n_batch, seq, name):
    n_chunks = seq // CHUNK
    gd = SGU_GROUP_DIM

    def body(u_ref, v_ref, ws_ref, bs_ref, g_ref, b_ref, o_ref, vo_ref):
        vn = _ln_rows(v_ref[...], g_ref[...], b_ref[...])
        vo_ref[...] = vn
        vb = vn.astype(BF16)
        row = lax.broadcasted_iota(jnp.int32, (CHUNK, CHUNK), 0)
        col = lax.broadcasted_iota(jnp.int32, (CHUNK, CHUNK), 1)
        keep = col <= row
        for grp in range(N_SGU_GROUPS):
            cols = slice(grp * gd, (grp + 1) * gd)
            wm = jnp.where(keep, ws_ref[grp], 0.0).astype(BF16)
            gate = jnp.dot(wm, vb[:, cols], preferred_element_type=F32) + bs_ref[:, grp:grp + 1]
            o_ref[:, cols] = (u_ref[:, cols] * gate).astype(o_ref.dtype)

    nb = D_SGU // D_SGU
    in_specs = [
        pl.BlockSpec((CHUNK, D_SGU), lambda b, n: (b * n_chunks + n, 0)),
        pl.BlockSpec((CHUNK, D_SGU), lambda b, n: (b * n_chunks + n, nb)),
        pl.BlockSpec((N_SGU_GROUPS, CHUNK, CHUNK), lambda b, n: (0, 0, 0)),
        pl.BlockSpec((CHUNK, N_SGU_GROUPS), lambda b, n: (0, 0)),
        pl.BlockSpec((1, D_SGU), lambda b, n: (0, 0)),
        pl.BlockSpec((1, D_SGU), lambda b, n: (0, 0)),
    ]
    out_specs = [pl.BlockSpec((CHUNK, D_SGU), lambda b, n: (b * n_chunks + n, 0)),
                 pl.BlockSpec((None, CHUNK, D_SGU), lambda b, n: (b, 0, 0))]
    out_shape = [jax.ShapeDtypeStruct((n_batch * seq, D_SGU), BF16),
                 jax.ShapeDtypeStruct((n_batch, CHUNK, D_SGU), F32)]
    return pl.pallas_call(body, grid=(n_batch, n_chunks), in_specs=in_specs, out_specs=out_specs,
                          out_shape=out_shape, compiler_params=_params(("parallel", "arbitrary")),
                          name=name)(uv, uv, w_s, b_s_t, g_v, b_v)


def _sgu_gate_decode(uv, w_vec, b_vec, g_v, b_v, *, name):
    n_b = uv.shape[0]

    def body(u_ref, v_ref, w_ref, bs_ref, g_ref, b_ref, o_ref, vo_ref):
        vn = _ln_rows(v_ref[...], g_ref[...], b_ref[...])
        vo_ref[...] = vn
        o_ref[...] = (u_ref[...] * (vn * w_ref[...] + bs_ref[...])).astype(o_ref.dtype)

    vec = pl.BlockSpec((1, D_SGU), lambda i: (0, 0))
    return pl.pallas_call(
        body, grid=(1,),
        in_specs=[pl.BlockSpec((n_b, D_SGU), lambda i: (0, 0)), pl.BlockSpec((n_b, D_SGU), lambda i: (0, 1)),
                  vec, vec, vec, vec],
        out_specs=[pl.BlockSpec((n_b, D_SGU), lambda i: (0, 0)), pl.BlockSpec((n_b, D_SGU), lambda i: (0, 0))],
        out_shape=[jax.ShapeDtypeStruct((n_b, D_SGU), BF16), jax.ShapeDtypeStruct((n_b, D_SGU), F32)],
        compiler_params=_params(("arbitrary",)), name=name)(uv, uv, w_vec, b_vec, g_v, b_v)


def _router(x, mod, w_r, b_r, l, *, tm, name):
    m_rows = x.shape[0]
    marr, rows_per_b = mod
    r = marr.shape[1]
    n_e = N_EXPERTS

    def body(x_ref, sh_ref, sc_ref, w_ref, b_ref, h_ref, gates_ref, idx_ref):
        h = x_ref[...] * (1.0 + sc_ref[...]) + sh_ref[...]
        h_ref[...] = h
        logits = jnp.dot(h, w_ref[...], preferred_element_type=F32,
                         precision=lax.Precision.HIGHEST) + b_ref[...]
        lane = lax.broadcasted_iota(jnp.int32, logits.shape, 1)
        m1 = jnp.max(logits, axis=-1, keepdims=True)
        i1 = jnp.min(jnp.where(logits == m1, lane, n_e), axis=-1, keepdims=True)
        rest = jnp.where(lane == i1, -jnp.inf, logits)
        m2 = jnp.max(rest, axis=-1, keepdims=True)
        i2 = jnp.min(jnp.where(rest == m2, lane, n_e), axis=-1, keepdims=True)
        e2 = jnp.exp(m2 - m1)
        den = 1.0 + e2
        gates_ref[...] = jnp.where(lane == i1, 1.0 / den, 0.0) + jnp.where(lane == i2, e2 / den, 0.0)
        idx_ref[...] = jnp.where(lane == 0, i1, jnp.where(lane == 1, i2, 0))

    in_specs = [
        pl.BlockSpec((tm, D_MODEL), lambda i: (i, 0)),
        pl.BlockSpec((None, r, D_MODEL), lambda i: ((i * tm) // rows_per_b, 0, 3)),
        pl.BlockSpec((None, r, D_MODEL), lambda i: ((i * tm) // rows_per_b, 0, 4)),
        pl.BlockSpec((None, D_MODEL, n_e), lambda i: (l, 0, 0)),
        pl.BlockSpec((None, 1, n_e), lambda i: (l, 0, 0)),
    ]
    out_specs = [pl.BlockSpec((tm, D_MODEL), lambda i: (i, 0)),
                 pl.BlockSpec((tm, n_e), lambda i: (i, 0)),
                 pl.BlockSpec((tm, n_e), lambda i: (i, 0))]
    out_shape = [jax.ShapeDtypeStruct((m_rows, D_MODEL), F32),
                 jax.ShapeDtypeStruct((m_rows, n_e), F32),
                 jax.ShapeDtypeStruct((m_rows, n_e), jnp.int32)]
    return pl.pallas_call(body, grid=(m_rows // tm,), in_specs=in_specs, out_specs=out_specs,
                          out_shape=out_shape, compiler_params=_params(("parallel",)), name=name)(
        x, marr, marr, w_r, b_r)


def _moe_combine(expert_out, pos, wts, x, mod, ln_g, ln_b, ln_idx, *, tm, name):
    m_rows = x.shape[0]
    marr, rows_per_b = mod
    r = marr.shape[1]

    def body(pos_ref, eo_ref, w_ref, x_ref, gt_ref, g_ref, b_ref, o_ref, buf_ref, sem):
        base = pl.program_id(0) * tm

        def issue(t, c):
            for k in range(TOP_K):
                src = pos_ref[(base + t) * TOP_K + k]
                pltpu.make_async_copy(eo_ref.at[pl.ds(src, 1)], buf_ref.at[k, pl.ds(t, 1)], sem).start()
            return c

        lax.fori_loop(0, tm, issue, 0)

        def wait(t, c):
            for k in range(TOP_K):
                pltpu.make_async_copy(eo_ref.at[pl.ds(0, 1)], buf_ref.at[k, pl.ds(t, 1)], sem).wait()
            return c

        lax.fori_loop(0, tm, wait, 0)
        w = w_ref[...]
        f = w[:, 0:1] * buf_ref[0] + w[:, 1:2] * buf_ref[1]
        o_ref[...] = _ln_rows(ALPHA * x_ref[...] + (1.0 + gt_ref[...]) * f, g_ref[...], b_ref[...])

    in_specs = [
        pl.BlockSpec(memory_space=pl.ANY),
        pl.BlockSpec((tm, TOP_K), lambda i, p: (i, 0)),
        pl.BlockSpec((tm, D_MODEL), lambda i, p: (i, 0)),
        pl.BlockSpec((None, r, D_MODEL), lambda i, p: ((i * tm) // rows_per_b, 0, 5)),
        pl.BlockSpec((None, 1, D_MODEL), lambda i, p: (ln_idx, 0, 0)),
        pl.BlockSpec((None, 1, D_MODEL), lambda i, p: (ln_idx, 0, 0)),
    ]
    grid_spec = pltpu.PrefetchScalarGridSpec(
        num_scalar_prefetch=1, grid=(m_rows // tm,), in_specs=in_specs,
        out_specs=pl.BlockSpec((tm, D_MODEL), lambda i, p: (i, 0)),
        scratch_shapes=[pltpu.VMEM((TOP_K, tm, D_MODEL), F32), pltpu.SemaphoreType.DMA(())])
    return pl.pallas_call(body, grid_spec=grid_spec, out_shape=jax.ShapeDtypeStruct(x.shape, F32),
                          compiler_params=_params(("parallel",)), name=name)(
        pos, expert_out, wts, x, marr, ln_g, ln_b)


def _route_tables(idx, n_tokens, tm):
    n_pairs = n_tokens * TOP_K
    n_tiles = -(-(n_pairs + N_EXPERTS * (tm - 1)) // tm)
    e_flat = idx[:, :TOP_K].reshape(-1)
    onehot = (e_flat[:, None] == jnp.arange(N_EXPERTS, dtype=jnp.int32)[None, :]).astype(jnp.int32)
    rank = jnp.take_along_axis(jnp.cumsum(onehot, axis=0), e_flat[:, None], axis=1)[:, 0] - 1
    counts = jnp.sum(onehot, axis=0)
    tiles_per = (counts + tm - 1) // tm
    tile_end = jnp.cumsum(tiles_per)
    tile_start = tile_end - tiles_per
    pos = (tile_start[e_flat] * tm + rank).astype(jnp.int32)
    row_src = jnp.zeros((n_tiles * tm,), jnp.int32).at[pos].set(
        jnp.arange(n_pairs, dtype=jnp.int32) // TOP_K)
    n_used = tile_end[-1].astype(jnp.int32)
    tile_ids = jnp.arange(n_tiles, dtype=jnp.int32)
    tile_e = jnp.sum((tile_ids[:, None] >= tile_end[None, :]).astype(jnp.int32), axis=1)
    last_e = jnp.sum((n_used - 1 >= tile_end).astype(jnp.int32))
    tile_e = jnp.where(tile_ids < n_used, tile_e, last_e).astype(jnp.int32)
    return row_src, tile_e, n_used.reshape(1), pos


def _rope_tables(pos, width_blocks):
    half = QK_ROPE_DIM // 2
    inv = ROPE_THETA ** (-jnp.arange(half, dtype=F32) / half)
    ang = pos.astype(F32)[:, None] * inv[None, :]
    zeros = jnp.zeros((pos.shape[0], LANES - QK_ROPE_DIM), F32)
    ta = jnp.concatenate([jnp.cos(ang), jnp.cos(ang), zeros], axis=1)
    tb = jnp.concatenate([jnp.sin(ang), jnp.sin(ang), zeros], axis=1)
    return jnp.tile(ta, (1, width_blocks)), jnp.tile(tb, (1, width_blocks))


def _rot_cols(w):
    half = QK_ROPE_DIM // 2
    return jnp.concatenate([-w[..., half:], w[..., :half]], axis=-1)


def _pad_lanes(w):
    return jnp.concatenate([w, jnp.zeros(w.shape[:-1] + (LANES - w.shape[-1],), w.dtype)], axis=-1)


def kernel(x_prompt, x_sample, cache_latent, cache_krope, state_conv, page_table, c_prompt, c_sample,
           w_ada, b_ada, ln_g, ln_b,
           mla_w_dqkv, mla_g_q, mla_g_kv, mla_w_uq, mla_w_qr, mla_w_uk, mla_w_uv, mla_w_o,
           conv_w_pw1, conv_b_pw1, conv_w_dw, conv_b_dw, conv_g_n, conv_b_n, conv_w_pw2, conv_b_pw2,
           sgu_w_in, sgu_b_in, sgu_g_v, sgu_b_v, sgu_w_s, sgu_b_s, sgu_w_out, sgu_b_out,
           ffn_w_gu, ffn_w_down, moe_w_r, moe_b_r, moe_w_gu, moe_w_down):
    n_batch, seq, d = x_prompt.shape
    n_dec = x_sample.shape[0]
    assert x_sample.shape[1] == 1 and d == D_MODEL
    past = page_table.shape[1] * cache_latent.shape[2]
    t_p = n_batch * seq
    tm_p = min(512, seq)
    tm_ff = min(1024, seq)
    tm_moe = min(512, seq)

    n_c = n_batch + n_dec
    n_c_pad = -(-n_c // 16) * 16
    c_all = jnp.concatenate([c_prompt, c_sample, jnp.zeros((n_c_pad - n_c, d), F32)], axis=0)
    b_ada3 = b_ada.reshape(DEPTH, 1, N_MOD * d)
    mods_p, mods_s = [], []
    for l in range(DEPTH):
        tn = 1024
        (mod_l,) = _proj(
            c_all, [(w_ada, (l,), 0)], lambda ys, es: ys,
            [((n_c_pad, N_MOD * d), F32, (n_c_pad, tn), lambda i, j: (0, j))],
            tm=n_c_pad, tn=tn, nj=N_MOD * d // tn, bs=[(b_ada3, l, 0)], silu_in=True, name=f"ada{l}")
        mods_p.append(mod_l[:n_batch].reshape(n_batch, 1, N_MOD * d))
        mods_s.append(mod_l[n_batch:n_c].reshape(1, n_dec, N_MOD * d))

    ln_g3 = ln_g.reshape(DEPTH * 2, 1, d)
    ln_b3 = ln_b.reshape(DEPTH * 2, 1, d)

    n_a = mla_w_dqkv.shape[0]
    hq = N_HEADS * QK_NOPE_DIM
    w_uq2 = mla_w_uq.reshape(n_a, Q_LORA_RANK, hq)
    w_uk2 = mla_w_uk.reshape(n_a, KV_LORA_RANK, hq)
    w_uv2 = mla_w_uv.reshape(n_a, KV_LORA_RANK, N_HEADS * V_HEAD_DIM)
    w_qr_a = _pad_lanes(mla_w_qr).reshape(n_a, Q_LORA_RANK, N_HEADS * LANES)
    w_qr_b = _pad_lanes(_rot_cols(mla_w_qr)).reshape(n_a, Q_LORA_RANK, N_HEADS * LANES)
    w_kr = mla_w_dqkv[:, :, Q_LORA_RANK + KV_LORA_RANK:]
    w_kr_a = _pad_lanes(w_kr)
    w_kr_b = _pad_lanes(_rot_cols(w_kr))
    g_q3 = mla_g_q.reshape(n_a, 1, Q_LORA_RANK)
    g_kv3 = mla_g_kv.reshape(n_a, 1, KV_LORA_RANK)
    n_ff = ffn_w_gu.shape[0]
    n_moe = moe_w_r.shape[0]
    b_r3 = moe_b_r.reshape(n_moe, 1, N_EXPERTS)

    def mla_front(x, mod, rows_per_b, tm, tabs, tabs_wide, j, tag):
        ta, tb = tabs
        c_q, c_kv, kr = _mla_down(x, (mod, rows_per_b), mla_w_dqkv, j, w_kr_a[j], w_kr_b[j],
                                  g_q3, g_kv3, ta, tb, tm=tm, name=f"mla_down_{tag}")
        m_rows = x.shape[0]
        tn = 512
        n_tab = ta.shape[0] // tm
        (qn,) = _proj(c_q, [(w_uq2, (j,), 0)], lambda ys, es: ys,
                      [((m_rows, hq), BF16 if tag[0] == 'p' else F32, (tm, tn), lambda i, jj: (i, jj))],
                      tm=tm, tn=tn, nj=hq // tn, name=f"mla_qn_{tag}")
        taw, tbw = tabs_wide
        (qr,) = _proj(c_q, [(w_qr_a, (j,), 0), (w_qr_b, (j,), 0)],
                      lambda ys, es: [ys[0] * es[0] + ys[1] * es[1]],
                      [((m_rows, hq), BF16 if tag[0] == 'p' else F32, (tm, tn), lambda i, jj: (i, jj))],
                      tm=tm, tn=tn, nj=hq // tn,
                      extras=[(taw, (tm, tn), lambda i, jj: (i % n_tab, 0)),
                              (tbw, (tm, tn), lambda i, jj: (i % n_tab, 0))],
                      name=f"mla_qr_{tag}")
        return c_q, c_kv, kr, qn, qr

    def res_args(x, mod, comp, rows_per_b, l, s):
        return (x, mod, comp, rows_per_b, ln_g3, ln_b3, 2 * l + s)

    def dense_ffn(x, mod, rows_per_b, tm_a, tm_b, l, tag):
        jf = l // 2
        tn = 256
        nb = DENSE_FF // tn
        (act,) = _proj(x, [(ffn_w_gu, (jf,), 0), (ffn_w_gu, (jf,), nb)],
                       lambda ys, es: [ys[0] * jax.nn.sigmoid(ys[0]) * ys[1]],
                       [((x.shape[0], DENSE_FF), BF16, (tm_a, tn), lambda i, jj: (i, jj))],
                       tm=tm_a, tn=tn, nj=nb, mod=(mod, 3, 4, rows_per_b), name=f"ffn_gu_{tag}")
        return _proj_k(act, ffn_w_down, (jf,), tm=tm_b, tk=512,
                       res=res_args(x, mod, 5, rows_per_b, l, 1), name=f"ffn_down_{tag}")

    pos_p = jnp.arange(seq, dtype=jnp.int32)
    pos_s = jnp.full((n_dec,), past, dtype=jnp.int32)
    tabs_p, tabs_s = _rope_tables(pos_p, 1), _rope_tables(pos_s, 1)
    tabs_pw, tabs_sw = _rope_tables(pos_p, 4), _rope_tables(pos_s, 4)

    xp = x_prompt.reshape(t_p, d)
    xs = x_sample.reshape(n_dec, d)
    lat_p, kr_p, lat_s, kr_s = [], [], [], []
    conv_p = conv_s = v_p = v_s = None

    for l in range(DEPTH):
        kind, j = l % 3, l // 3
        mp, ms = mods_p[l], mods_s[l]
        if kind == 0:
            c_q, c_kv, kr, qn, qr = mla_front(xp, mp, seq, tm_p, tabs_p, tabs_pw, j, f"p{l}")
            tn = 512
            (kn,) = _proj(c_kv, [(w_uk2, (j,), 0)], lambda ys, es: ys,
                          [((t_p, hq), BF16, (tm_p, tn), lambda i, jj: (i, jj))],
                          tm=tm_p, tn=tn, nj=hq // tn, name=f"mla_kn_p{l}")
            (vv,) = _proj(c_kv, [(w_uv2, (j,), 0)], lambda ys, es: ys,
                          [((t_p, hq), BF16, (tm_p, tn), lambda i, jj: (i, jj))],
                          tm=tm_p, tn=tn, nj=hq // tn, name=f"mla_v_p{l}")
            o = _attn_prompt(qn, qr, kn, kr, vv, n_batch=n_batch, seq=seq, tq=min(512, seq),
                             name=f"attn_p{l}")
            xp = _proj_k(o, mla_w_o, (j,), tm=tm_p, tk=512, res=res_args(xp, mp, 2, seq, l, 0),
                         name=f"mla_o_p{l}")
            lat_p.append(c_kv.reshape(n_batch, seq, KV_LORA_RANK))
            kr_p.append(kr[:, :QK_ROPE_DIM].reshape(n_batch, seq, QK_ROPE_DIM))
            c_q, c_kv, kr, qn, qr = mla_front(xs, ms, n_dec, n_dec, tabs_s, tabs_sw, j, f"s{l}")
            q_lat = _head_mix(qn, w_uk2, j, contract_w_last=True, name=f"mla_qlat_s{l}")
            o_lat = _attn_decode(q_lat.reshape(n_dec, N_HEADS, KV_LORA_RANK),
                                 qr.reshape(n_dec, N_HEADS, LANES),
                                 c_kv.reshape(n_dec, 1, KV_LORA_RANK), kr.reshape(n_dec, 1, LANES),
                                 cache_latent, cache_krope, page_table, j, name=f"attn_s{l}")
            o = _head_mix(o_lat.reshape(n_dec, N_HEADS * KV_LORA_RANK), w_uv2, j,
                          contract_w_last=False, name=f"mla_ov_s{l}")
            xs = _proj_k(o, mla_w_o, (j,), tm=n_dec, tk=512, res=res_args(xs, ms, 2, n_dec, l, 0),
                         name=f"mla_o_s{l}")
            lat_s.append(c_kv.reshape(n_dec, 1, KV_LORA_RANK))
            kr_s.append(kr[:, :QK_ROPE_DIM].reshape(n_dec, 1, QK_ROPE_DIM))
        elif kind == 1:
            d_c = conv_w_pw1.shape[2] // 2
            b1 = conv_b_pw1.reshape(-1, 1, 2 * d_c)
            b2 = conv_b_pw2.reshape(-1, 1, d)
            tn = 512
            glu = lambda ys, es: [ys[0] * jax.nn.sigmoid(ys[1])]

            def pw1(x, mod, rows_per_b, tm, tag):
                return _proj(x, [(conv_w_pw1, (j,), 0), (conv_w_pw1, (j,), d_c // tn)], glu,
                             [((x.shape[0], d_c), F32, (tm, tn), lambda i, jj: (i, jj))],
                             tm=tm, tn=tn, nj=d_c // tn, bs=[(b1, j, 0), (b1, j, d_c // tn)],
                             mod=(mod, 0, 1, rows_per_b), name=f"conv_pw1_{tag}")[0]

            u = pw1(xp, mp, seq, tm_p, f"p{l}")
            y = _dwconv_prompt(u, conv_w_dw[j], conv_b_dw[j].reshape(1, d_c), conv_g_n[j].reshape(1, d_c),
                               conv_b_n[j].reshape(1, d_c), n_batch=n_batch, seq=seq, tm=tm_p,
                               name=f"dwconv_p{l}")
            xp = _proj_k(y, conv_w_pw2, (j,), tm=tm_p, tk=512, bias=(b2, j),
                         res=res_args(xp, mp, 2, seq, l, 0), name=f"conv_pw2_p{l}")
            conv_p = u.reshape(n_batch, seq, d_c)[:, seq - (CONV_WIDTH - 1):][None]

            u = pw1(xs, ms, n_dec, n_dec, f"s{l}")
            y = _dwconv_decode(u, state_conv[j], conv_w_dw[j], conv_b_dw[j].reshape(1, d_c),
                               conv_g_n[j].reshape(1, d_c), conv_b_n[j].reshape(1, d_c), name=f"dwconv_s{l}")
            xs = _proj_k(y, conv_w_pw2, (j,), tm=n_dec, tk=512, bias=(b2, j),
                         res=res_args(xs, ms, 2, n_dec, l, 0), name=f"conv_pw2_s{l}")
            conv_s = jnp.concatenate([state_conv[j][:, 1:], u[:, None, :]], axis=1)[None]
        else:
            b_in = sgu_b_in.reshape(-1, 1, 2 * D_SGU)
            b_out = sgu_b_out.reshape(-1, 1, d)
            tn = 512
            gelu = lambda ys, es: [_gelu_tanh(ys[0])]

            def sgu_in(x, mod, rows_per_b, tm, tag):
                return _proj(x, [(sgu_w_in, (j,), 0)], gelu,
                             [((x.shape[0], 2 * D_SGU), F32, (tm, tn), lambda i, jj: (i, jj))],
                             tm=tm, tn=tn, nj=2 * D_SGU // tn, bs=[(b_in, j, 0)],
                             mod=(mod, 0, 1, rows_per_b), name=f"sgu_in_{tag}")[0]

            g_v = sgu_g_v[j].reshape(1, D_SGU)
            b_v = sgu_b_v[j].reshape(1, D_SGU)
            uv = sgu_in(xp, mp, seq, tm_p, f"p{l}")
            ug, v_last = _sgu_gate_prompt(uv, sgu_w_s[j], sgu_b_s[j].T, g_v, b_v, n_batch=n_batch, seq=seq,
                                          name=f"sgu_gate_p{l}")
            xp = _proj_k(ug, sgu_w_out, (j,), tm=tm_p, tk=512, bias=(b_out, j),
                         res=res_args(xp, mp, 2, seq, l, 0), name=f"sgu_out_p{l}")
            v_p = v_last[None]

            uv = sgu_in(xs, ms, n_dec, n_dec, f"s{l}")
            w_vec = jnp.repeat(sgu_w_s[j][:, 0, 0], SGU_GROUP_DIM).reshape(1, D_SGU)
            b_vec = jnp.repeat(sgu_b_s[j][:, 0], SGU_GROUP_DIM).reshape(1, D_SGU)
            ug, vn = _sgu_gate_decode(uv, w_vec, b_vec, g_v, b_v, name=f"sgu_gate_s{l}")
            xs = _proj_k(ug, sgu_w_out, (j,), tm=n_dec, tk=512, bias=(b_out, j),
                         res=res_args(xs, ms, 2, n_dec, l, 0), name=f"sgu_out_s{l}")
            v_s = vn.reshape(n_dec, 1, D_SGU)[None]

        if l % 2 == 0:
            xp = dense_ffn(xp, mp, seq, tm_ff, tm_p, l, f"p{l}")
            xs = dense_ffn(xs, ms, n_dec, n_dec, n_dec, l, f"s{l}")
        else:
            jm = l // 2
            h_p, gates_p, idx_p = _router(xp, (mp, seq), moe_w_r, b_r3, jm, tm=tm_p, name=f"router_p{l}")
            h_s, gates_s, idx_s = _router(xs, (ms, n_dec), moe_w_r, b_r3, jm, tm=n_dec, name=f"router_s{l}")
            h_all = jnp.concatenate([h_p, h_s], axis=0)
            idx_all = jnp.concatenate([idx_p, idx_s], axis=0)
            gates_all = jnp.concatenate([gates_p, gates_s], axis=0)
            n_tok = t_p + n_dec
            row_src, tile_e, n_used, pos = _route_tables(idx_all, n_tok, tm_moe)
            wts = jnp.take_along_axis(gates_all, idx_all[:, :TOP_K], axis=1)
            n_rows = row_src.shape[0]
            tn = 512
            nb = EXPERT_FF // tn
            (act,) = _proj(h_all, [(moe_w_gu, (jm, 'e'), 0), (moe_w_gu, (jm, 'e'), nb)],
                           lambda ys, es: [ys[0] * jax.nn.sigmoid(ys[0]) * ys[1]],
                           [((n_rows, EXPERT_FF), BF16, (tm_moe, tn), lambda i, jj: (i, jj))],
                           tm=tm_moe, tn=tn, nj=nb, gather=row_src, expert=(tile_e, n_used),
                           name=f"moe_gu{l}")
            eo = _proj_k(act, moe_w_down, (jm, 'e'), tm=tm_moe, tk=512, expert=(tile_e, n_used),
                         name=f"moe_down{l}")
            xp = _moe_combine(eo, pos[:t_p * TOP_K], wts[:t_p], xp, (mp, seq), ln_g3, ln_b3, 2 * l + 1,
                              tm=tm_p, name=f"moe_comb_p{l}")
            xs = _moe_combine(eo, pos[t_p * TOP_K:], wts[t_p:], xs, (ms, n_dec), ln_g3, ln_b3, 2 * l + 1,
                              tm=n_dec, name=f"moe_comb_s{l}")

    return (xp.reshape(n_batch, seq, d), xs.reshape(n_dec, 1, d),
            jnp.stack(lat_p), jnp.stack(kr_p), conv_p, v_p,
            jnp.stack(lat_s), jnp.stack(kr_s), conv_s, v_s)
```

```python
import functools

import jax
import jax.numpy as jnp
from jax import lax
from jax.experimental import pallas as pl
from jax.experimental.pallas import tpu as pltpu

F32 = jnp.float32
BF16 = jnp.bfloat16

D_MODEL = 2048
DEPTH = 4
PAGE_SIZE = 128
N_HEADS = 16
QK_NOPE_DIM = 128
QK_ROPE_DIM = 64
V_HEAD_DIM = 128
Q_LORA_RANK = 512
KV_LORA_RANK = 512
ROPE_THETA = 10000.0
ATTN_SCALE = (QK_NOPE_DIM + QK_ROPE_DIM) ** -0.5
CONV_WIDTH = 31
CHUNK = 128
D_SGU = 2 * D_MODEL
N_SGU_GROUPS = 16
SGU_GROUP_DIM = D_SGU // N_SGU_GROUPS
DENSE_FF = 5632
N_EXPERTS = 8
TOP_K = 2
EXPERT_FF = 7168
N_MOD = 6
ALPHA = (2 * DEPTH) ** 0.25
LN_EPS = 1e-5
RMS_EPS = 1e-6

LANES = 128
V7X_VMEM_LIMIT = 56 * 1024 * 1024
CONV_HALO = 32
PAGES_PER_STEP = 16
SEQS_PER_STEP = 2


def _params(sem):
    return pltpu.CompilerParams(dimension_semantics=sem, vmem_limit_bytes=V7X_VMEM_LIMIT)


def _ln_rows(v, g, b):
    mu = jnp.mean(v, axis=-1, keepdims=True)
    vc = v - mu
    var = jnp.mean(vc * vc, axis=-1, keepdims=True)
    return vc * lax.rsqrt(var + LN_EPS) * g + b


def _gelu_tanh(y):
    return 0.5 * y * (1.0 + jnp.tanh(0.7978845608028654 * (y + 0.044715 * (y * y * y))))


def _proj(x, ws, epi, outs, *, tm, tn, nj, bs=None, mod=None, extras=(), silu_in=False,
          expert=None, cols_outer=False, name):
    if bs is None:
        bs = [None] * len(ws)
    m_rows, k_dim = x.shape
    ni = m_rows // tm
    assert ni * tm == m_rows
    has_mod = mod is not None
    prefetch = list(expert) if expert is not None else []
    n_pf = len(prefetch)
    direct = (x.dtype == BF16 and not has_mod and not silu_in)
    assert direct or not cols_outer

    def order(f):
        if cols_outer:
            return lambda j, i, *pf: f(i, j, *pf)
        return f

    def x_imap(i, j, *pf):
        if expert is not None:
            i = jnp.minimum(i, pf[1][0] - 1)
        return (i, 0)

    def w_imap(lead, off):
        def f(i, j, *pf):
            jj = j
            if expert is not None and not cols_outer:
                jj = jnp.where(i < pf[1][0], j, nj - 1)
            return tuple(pf[0][i] if s == 'e' else s for s in lead) + (0, off + jj)
        return f

    in_specs = [pl.BlockSpec((tm, k_dim), order(x_imap))]
    args = [x]
    if has_mod:
        marr, c_sh, c_sc, rows_per_b = mod
        r = marr.shape[1]
        for comp in (c_sh, c_sc):
            in_specs.append(pl.BlockSpec(
                (None, r, k_dim), order(lambda i, j, *pf, comp=comp: ((i * tm) // rows_per_b, 0, comp))))
            args.append(marr)
    for arr, lead, off in ws:
        in_specs.append(pl.BlockSpec((None,) * len(lead) + (k_dim, tn), order(w_imap(lead, off))))
        args.append(arr)
    for b in bs:
        if b is not None:
            arr, lead, off = b
            in_specs.append(pl.BlockSpec(
                (None, 1, tn), order(lambda i, j, *pf, lead=lead, off=off: (lead, 0, off + j))))
            args.append(arr)
    for arr, bshape, imap in extras:
        in_specs.append(pl.BlockSpec(bshape, order(lambda i, j, *pf, imap=imap: imap(i, j))))
        args.append(arr)
    out_specs = [pl.BlockSpec(bshape, order(lambda i, j, *pf, imap=imap: imap(i, j)))
                 for (_, _, bshape, imap) in outs]
    out_shape = [jax.ShapeDtypeStruct(shape, dtype) for (shape, dtype, _, _) in outs]
    scratch = [] if direct else [pltpu.VMEM((tm, k_dim), BF16)]

    def body(*refs):
        it = iter(refs)
        pf = [next(it) for _ in range(n_pf)]
        x_ref = next(it)
        sh_ref = sc_ref = None
        if has_mod:
            sh_ref, sc_ref = next(it), next(it)
        w_refs = [next(it) for _ in ws]
        b_refs = [next(it) if b is not None else None for b in bs]
        e_refs = [next(it) for _ in extras]
        o_refs = [next(it) for _ in outs]
        h_ref = None if direct else next(it)
        i, j = pl.program_id(0), pl.program_id(1)
        if cols_outer:
            i, j = j, i

        if not direct:
            @pl.when(j == 0)
            def _():
                h = x_ref[...].astype(F32)
                if has_mod:
                    h = h * (1.0 + sc_ref[...]) + sh_ref[...]
                if silu_in:
                    h = h * jax.nn.sigmoid(h)
                h_ref[...] = h.astype(BF16)

        def compute():
            h = x_ref[...] if direct else h_ref[...]
            ys = []
            for w_ref, b_ref in zip(w_refs, b_refs):
                y = jnp.dot(h, w_ref[...].astype(BF16), preferred_element_type=F32)
                if b_ref is not None:
                    y = y + b_ref[...]
                ys.append(y)
            res = epi(ys, [e[...] for e in e_refs])
            for o_ref, r_ in zip(o_refs, res):
                o_ref[...] = r_.astype(o_ref.dtype)

        if expert is not None:
            @pl.when(i < pf[1][0])
            def _():
                compute()

            @pl.when(i >= pf[1][0])
            def _():
                for o_ref in o_refs:
                    o_ref[...] = jnp.zeros(o_ref.shape, o_ref.dtype)
        else:
            compute()

    grid = (nj, ni) if cols_outer else (ni, nj)
    sem = ("arbitrary", "arbitrary") if cols_outer else ("parallel", "arbitrary")
    grid_spec = pltpu.PrefetchScalarGridSpec(
        num_scalar_prefetch=n_pf, grid=grid, in_specs=in_specs, out_specs=out_specs,
        scratch_shapes=scratch)
    return pl.pallas_call(body, grid_spec=grid_spec, out_shape=out_shape,
                          compiler_params=_params(sem), name=name)(*prefetch, *args)


def _proj_k(a, w, lead, *, tm, tk, res=None, bias=None, expert=None, name):
    m_rows, k_dim = a.shape
    n_dim = w.shape[-1]
    ni, nk = m_rows // tm, k_dim // tk
    assert ni * tm == m_rows and nk * tk == k_dim
    prefetch = list(expert) if expert is not None else []
    n_pf = len(prefetch)

    def a_imap(i, k, *pf):
        kk = k
        if expert is not None:
            kk = jnp.where(i < pf[1][0], k, nk - 1)
        return (i, kk)

    def w_imap(i, k, *pf):
        kk = k
        if expert is not None:
            kk = jnp.where(i < pf[1][0], k, nk - 1)
        return tuple(pf[0][i] if s == 'e' else s for s in lead) + (kk, 0)

    in_specs = [pl.BlockSpec((tm, tk), a_imap),
                pl.BlockSpec((None,) * len(lead) + (tk, n_dim), w_imap)]
    args = [a, w]
    if bias is not None:
        barr, bl = bias
        in_specs.append(pl.BlockSpec((None, 1, n_dim), lambda i, k, *pf: (bl, 0, 0)))
        args.append(barr)
    if res is not None:
        x_res, marr, c_gate, rows_per_b, ln_g, ln_b, ln_idx = res
        r = marr.shape[1]
        in_specs += [
            pl.BlockSpec((tm, n_dim), lambda i, k, *pf: (i, 0)),
            pl.BlockSpec((None, r, n_dim), lambda i, k, *pf: ((i * tm) // rows_per_b, 0, c_gate)),
            pl.BlockSpec((None, 1, n_dim), lambda i, k, *pf: (ln_idx, 0, 0)),
            pl.BlockSpec((None, 1, n_dim), lambda i, k, *pf: (ln_idx, 0, 0)),
        ]
        args += [x_res, marr, ln_g, ln_b]

    def body(*refs):
        it = iter(refs)
        pf = [next(it) for _ in range(n_pf)]
        a_ref, w_ref = next(it), next(it)
        b_ref = next(it) if bias is not None else None
        if res is not None:
            x_ref, gt_ref, g_ref, be_ref = next(it), next(it), next(it), next(it)
        o_ref, acc_ref = next(it), next(it)
        i, k = pl.program_id(0), pl.program_id(1)

        def compute():
            @pl.when(k == 0)
            def _():
                acc_ref[...] = jnp.zeros(acc_ref.shape, F32)

            acc_ref[...] += jnp.dot(a_ref[...].astype(BF16), w_ref[...].astype(BF16),
                                    preferred_element_type=F32)

            @pl.when(k == nk - 1)
            def _():
                y = acc_ref[...]
                if b_ref is not None:
                    y = y + b_ref[...]
                if res is not None:
                    y = _ln_rows(ALPHA * x_ref[...] + (1.0 + gt_ref[...]) * y, g_ref[...], be_ref[...])
                o_ref[...] = y.astype(o_ref.dtype)

        if expert is not None:
            @pl.when(i < pf[1][0])
            def _():
                compute()

            @pl.when(jnp.logical_and(i >= pf[1][0], k == nk - 1))
            def _():
                o_ref[...] = jnp.zeros(o_ref.shape, o_ref.dtype)
        else:
            compute()

    grid_spec = pltpu.PrefetchScalarGridSpec(
        num_scalar_prefetch=n_pf, grid=(ni, nk), in_specs=in_specs,
        out_specs=pl.BlockSpec((tm, n_dim), lambda i, k, *pf: (i, 0)),
        scratch_shapes=[pltpu.VMEM((tm, n_dim), F32)])
    return pl.pallas_call(body, grid_spec=grid_spec,
                          out_shape=jax.ShapeDtypeStruct((m_rows, n_dim), F32),
                          compiler_params=_params(("parallel", "arbitrary")), name=name)(*prefetch, *args)


def _mla_down(x, mod, w_dqkv, l, w_ka, w_kb, g_q, g_kv, tab_a, tab_b, *, tm, name):
    m_rows = x.shape[0]
    ni = m_rows // tm
    marr, rows_per_b = mod
    r = marr.shape[1]
    n_tab = tab_a.shape[0] // tm

    def body(x_ref, sh_ref, sc_ref, wq_ref, wkv_ref, wka_ref, wkb_ref, gq_ref, gkv_ref, ta_ref, tb_ref,
             cq_ref, ckv_ref, kr_ref):
        h = (x_ref[...] * (1.0 + sc_ref[...]) + sh_ref[...]).astype(BF16)

        def rms(w_ref, g_ref):
            z = jnp.dot(h, w_ref[...].astype(BF16), preferred_element_type=F32)
            return z * lax.rsqrt(jnp.mean(z * z, axis=-1, keepdims=True) + RMS_EPS) * g_ref[...]

        cq_ref[...] = rms(wq_ref, gq_ref).astype(cq_ref.dtype)
        ckv_ref[...] = rms(wkv_ref, gkv_ref)
        ka = jnp.dot(h, wka_ref[...].astype(BF16), preferred_element_type=F32)
        kb = jnp.dot(h, wkb_ref[...].astype(BF16), preferred_element_type=F32)
        kr_ref[...] = ka * ta_ref[...] + kb * tb_ref[...]

    rank = Q_LORA_RANK
    in_specs = [
        pl.BlockSpec((tm, D_MODEL), lambda i: (i, 0)),
        pl.BlockSpec((None, r, D_MODEL), lambda i: ((i * tm) // rows_per_b, 0, 0)),
        pl.BlockSpec((None, r, D_MODEL), lambda i: ((i * tm) // rows_per_b, 0, 1)),
        pl.BlockSpec((None, D_MODEL, rank), lambda i: (l, 0, 0)),
        pl.BlockSpec((None, D_MODEL, rank), lambda i: (l, 0, 1)),
        pl.BlockSpec((D_MODEL, LANES), lambda i: (0, 0)),
        pl.BlockSpec((D_MODEL, LANES), lambda i: (0, 0)),
        pl.BlockSpec((None, 1, rank), lambda i: (l, 0, 0)),
        pl.BlockSpec((None, 1, rank), lambda i: (l, 0, 0)),
        pl.BlockSpec((tm, LANES), lambda i: (i % n_tab, 0)),
        pl.BlockSpec((tm, LANES), lambda i: (i % n_tab, 0)),
    ]
    out_specs = [pl.BlockSpec((tm, rank), lambda i: (i, 0)),
                 pl.BlockSpec((tm, rank), lambda i: (i, 0)),
                 pl.BlockSpec((tm, LANES), lambda i: (i, 0))]
    out_shape = [jax.ShapeDtypeStruct((m_rows, rank), BF16),
                 jax.ShapeDtypeStruct((m_rows, rank), F32),
                 jax.ShapeDtypeStruct((m_rows, LANES), F32)]
    return pl.pallas_call(body, grid=(ni,), in_specs=in_specs, out_specs=out_specs, out_shape=out_shape,
                          compiler_params=_params(("parallel",)), name=name)(
        x, marr, marr, w_dqkv, w_dqkv, w_ka, w_kb, g_q, g_kv, tab_a, tab_b)


def _attn_prompt(qn, qr, kn, kr, v, *, n_batch, seq, tq, name):
    hd = QK_NOPE_DIM
    n_q = seq // tq

    def body(qn_ref, qr_ref, kn_ref, kr_ref, v_ref, o_ref):
        k_all = jnp.concatenate([kn_ref[...], kr_ref[...].astype(BF16)], axis=1)
        for qi in range(n_q):
            rows = slice(qi * tq, (qi + 1) * tq)
            n_kv = (qi + 1) * tq
            q = jnp.concatenate([qn_ref[rows, :], qr_ref[rows, :]], axis=1)
            s = lax.dot_general(q, k_all[:n_kv], (((1,), (1,)), ((), ())),
                                preferred_element_type=F32) * ATTN_SCALE
            row = lax.broadcasted_iota(jnp.int32, (tq, n_kv), 0) + qi * tq
            col = lax.broadcasted_iota(jnp.int32, (tq, n_kv), 1)
            s = jnp.where(col <= row, s, -jnp.inf)
            m = jnp.max(s, axis=-1, keepdims=True)
            p = jnp.exp(s - m)
            den = jnp.sum(p, axis=-1, keepdims=True)
            o = jnp.dot(p.astype(BF16), v_ref[0:n_kv, :], preferred_element_type=F32) / den
            o_ref[rows, :] = o.astype(o_ref.dtype)

    hspec = pl.BlockSpec((seq, hd), lambda b, h: (b, h))
    in_specs = [hspec, hspec, hspec, pl.BlockSpec((seq, LANES), lambda b, h: (b, 0)), hspec]
    return pl.pallas_call(body, grid=(n_batch, N_HEADS), in_specs=in_specs, out_specs=hspec,
                          out_shape=jax.ShapeDtypeStruct(qn.shape, BF16),
                          compiler_params=_params(("parallel", "parallel")), name=name)(qn, qr, kn, kr, v)


def _head_mix(a, w, l, *, contract_w_last, name):
    n_b = a.shape[0]
    c_dim = w.shape[1]
    hd = QK_NOPE_DIM

    def body(a_ref, w_ref, o_ref):
        wv = w_ref[...].astype(BF16)
        av = a_ref[...].astype(BF16)
        if contract_w_last:
            o_ref[...] = lax.dot_general(av, wv, (((1,), (1,)), ((), ())), preferred_element_type=F32)
        else:
            o_ref[...] = jnp.dot(av, wv, preferred_element_type=F32)

    a_w, o_w = (hd, c_dim) if contract_w_last else (c_dim, hd)
    return pl.pallas_call(
        body, grid=(N_HEADS,),
        in_specs=[pl.BlockSpec((n_b, a_w), lambda h: (0, h)),
                  pl.BlockSpec((None, c_dim, hd), lambda h: (l, 0, h))],
        out_specs=pl.BlockSpec((n_b, o_w), lambda h: (0, h)),
        out_shape=jax.ShapeDtypeStruct((n_b, N_HEADS * o_w), F32),
        compiler_params=_params(("parallel",)), name=name)(a, w)


def _attn_decode(q_lat, q_rope, c_new, kr_new, cache_lat, cache_kr, page_table, l, *, name):
    n_b, n_pages = page_table.shape
    g, nq = PAGES_PER_STEP, SEQS_PER_STEP
    n_steps = n_pages // g
    assert n_steps * g == n_pages and n_b % nq == 0
    c_dim, r_dim = KV_LORA_RANK, QK_ROPE_DIM
    pt_flat = page_table.reshape(-1)
    n_pg = nq * g
    n_total = (n_b // nq) * n_steps

    def body(pt_ref, ql_ref, qr_ref, cn_ref, kn_ref, clat_ref, ckr_ref, o_ref,
             cbuf_ref, kbuf_ref, sems, m_ref, d_ref, acc_ref):
        s_idx = pl.program_id(1)
        step = pl.program_id(0) * n_steps + s_idx
        slot = step % 2
        nt = (((1,), (1,)), ((), ()))

        def page_copies(step_idx, slot_idx):
            b0 = (step_idx // n_steps) * nq
            s0 = (step_idx % n_steps) * g
            out = []
            for q in range(nq):
                for t in range(g):
                    page = pt_ref[(b0 + q) * n_pages + s0 + t]
                    out.append(pltpu.make_async_copy(
                        clat_ref.at[l, page], cbuf_ref.at[slot_idx, q * g + t], sems.at[slot_idx]))
                    out.append(pltpu.make_async_copy(
                        ckr_ref.at[l, page], kbuf_ref.at[slot_idx, q * g + t], sems.at[slot_idx]))
            return out

        @pl.when(step == 0)
        def _():
            for cp in page_copies(0, 0):
                cp.start()

        @pl.when(step + 1 < n_total)
        def _():
            for cp in page_copies(step + 1, 1 - slot):
                cp.start()

        for cp in page_copies(step, slot):
            cp.wait()
        c_refs = [cbuf_ref.at[slot, i] for i in range(n_pg)]
        kr_refs = [kbuf_ref.at[slot, i] for i in range(n_pg)]

        @pl.when(s_idx == 0)
        def _():
            for q in range(nq):
                cn = cn_ref[q]
                s_new = (jnp.sum(ql_ref[q] * cn, axis=-1, keepdims=True)
                         + jnp.sum(qr_ref[q][:, :r_dim] * kn_ref[q][:, :r_dim], axis=-1, keepdims=True))
                m_ref[q] = s_new * ATTN_SCALE
                d_ref[q] = jnp.ones((N_HEADS, 1), F32)
                acc_ref[q] = jnp.broadcast_to(cn, (N_HEADS, c_dim))

        for q in range(nq):
            qlb = ql_ref[q].astype(BF16)
            qrb = qr_ref[q][:, :r_dim].astype(BF16)
            cs = [c_refs[q * g + t][...].astype(BF16) for t in range(g)]
            s = jnp.concatenate(
                [lax.dot_general(qlb, cs[t], nt, preferred_element_type=F32)
                 + jnp.dot(qrb, kr_refs[q * g + t][...].astype(BF16), preferred_element_type=F32)
                 for t in range(g)], axis=1) * ATTN_SCALE
            m_old = m_ref[q]
            m_new = jnp.maximum(m_old, jnp.max(s, axis=-1, keepdims=True))
            scale_old = jnp.exp(m_old - m_new)
            p = jnp.exp(s - m_new)
            d_ref[q] = d_ref[q] * scale_old + jnp.sum(p, axis=-1, keepdims=True)
            pb = p.astype(BF16)
            pv = jnp.dot(pb[:, 0:PAGE_SIZE], cs[0], preferred_element_type=F32)
            for t in range(1, g):
                pv = pv + jnp.dot(pb[:, t * PAGE_SIZE:(t + 1) * PAGE_SIZE], cs[t],
                                  preferred_element_type=F32)
            acc_ref[q] = acc_ref[q] * scale_old + pv
            m_ref[q] = m_new

        @pl.when(s_idx == n_steps - 1)
        def _():
            o_ref[...] = acc_ref[...] / d_ref[...]

    def seq_spec(rows, width):
        return pl.BlockSpec((nq, rows, width), lambda b, s, pt: (b, 0, 0))

    in_specs = [seq_spec(N_HEADS, c_dim), seq_spec(N_HEADS, LANES), seq_spec(1, c_dim), seq_spec(1, LANES),
                pl.BlockSpec(memory_space=pl.ANY), pl.BlockSpec(memory_space=pl.ANY)]
    grid_spec = pltpu.PrefetchScalarGridSpec(
        num_scalar_prefetch=1, grid=(n_b // nq, n_steps), in_specs=in_specs,
        out_specs=seq_spec(N_HEADS, c_dim),
        scratch_shapes=[pltpu.VMEM((2, n_pg, PAGE_SIZE, c_dim), F32),
                        pltpu.VMEM((2, n_pg, r_dim, PAGE_SIZE), F32),
                        pltpu.SemaphoreType.DMA((2,)),
                        pltpu.VMEM((nq, N_HEADS, 1), F32), pltpu.VMEM((nq, N_HEADS, 1), F32),
                        pltpu.VMEM((nq, N_HEADS, c_dim), F32)])
    return pl.pallas_call(body, grid_spec=grid_spec,
                          out_shape=jax.ShapeDtypeStruct((n_b, N_HEADS, c_dim), F32),
                          compiler_params=_params(("arbitrary", "arbitrary")), name=name)(
        pt_flat, q_lat, q_rope, c_new, kr_new, cache_lat, cache_kr)


def _dwconv_prompt(u, w_dw, b_dw, g_n, b_n, *, n_batch, seq, tm, name):
    n_ch = u.shape[1]
    n_cc = n_ch // LANES
    nt = seq // tm
    halo_blocks = tm // CONV_HALO
    rc = 64
    first = CONV_HALO - (CONV_WIDTH - 1)

    def body(u_ref, halo_ref, w_ref, bd_ref, g_ref, b_ref, o_ref, ext_ref, y_ref):
        t, c = pl.program_id(1), pl.program_id(2)
        ext_ref[0:CONV_HALO, :] = jnp.where(t == 0, 0.0, halo_ref[...])
        ext_ref[CONV_HALO:, :] = u_ref[...]
        taps = [w_ref[k:k + 1, :] for k in range(CONV_WIDTH)]
        for r0 in range(0, tm, rc):
            acc = jnp.zeros((rc, LANES), F32)
            for k in range(CONV_WIDTH):
                acc = acc + taps[k] * ext_ref[r0 + first + k:r0 + first + k + rc, :]
            y_ref[c, r0:r0 + rc, :] = acc + bd_ref[...]

        @pl.when(c == n_cc - 1)
        def _():
            y = y_ref[...]
            mu = jnp.sum(jnp.sum(y, axis=0), axis=-1, keepdims=True) / n_ch
            yc = y - mu[None]
            var = jnp.sum(jnp.sum(yc * yc, axis=0), axis=-1, keepdims=True) / n_ch
            inv = lax.rsqrt(var + LN_EPS)
            for cc in range(n_cc):
                cols = slice(cc * LANES, (cc + 1) * LANES)
                z = yc[cc] * inv * g_ref[:, cols] + b_ref[:, cols]
                o_ref[:, cols] = (z * jax.nn.sigmoid(z)).astype(o_ref.dtype)

    def halo_imap(b, t, c):
        return (jnp.maximum((b * seq + t * tm) // CONV_HALO - 1, 0), c)

    in_specs = [
        pl.BlockSpec((tm, LANES), lambda b, t, c: (b * nt + t, c)),
        pl.BlockSpec((CONV_HALO, LANES), halo_imap),
        pl.BlockSpec((CONV_WIDTH, LANES), lambda b, t, c: (0, c)),
        pl.BlockSpec((1, LANES), lambda b, t, c: (0, c)),
        pl.BlockSpec((1, n_ch), lambda b, t, c: (0, 0)),
        pl.BlockSpec((1, n_ch), lambda b, t, c: (0, 0)),
    ]
    del halo_blocks
    return pl.pallas_call(
        body, grid=(n_batch, nt, n_cc), in_specs=in_specs,
        out_specs=pl.BlockSpec((tm, n_ch), lambda b, t, c: (b * nt + t, 0)),
        out_shape=jax.ShapeDtypeStruct(u.shape, BF16),
        scratch_shapes=[pltpu.VMEM((tm + CONV_HALO, LANES), F32), pltpu.VMEM((n_cc, tm, LANES), F32)],
        compiler_params=_params(("parallel", "parallel", "arbitrary")), name=name)(
        u, u, w_dw, b_dw, g_n, b_n)


def _dwconv_decode(u, buf, w_dw, b_dw, g_n, b_n, *, name):
    n_b, n_ch = u.shape
    bt = 8

    def body(u_ref, buf_ref, w_ref, bd_ref, g_ref, b_ref, o_ref):
        w = w_ref[...]
        y = jnp.sum(buf_ref[...] * w[None, :CONV_WIDTH - 1, :], axis=1)
        y = y + u_ref[...] * w[CONV_WIDTH - 1:CONV_WIDTH, :] + bd_ref[...]
        z = _ln_rows(y, g_ref[...], b_ref[...])
        o_ref[...] = (z * jax.nn.sigmoid(z)).astype(o_ref.dtype)

    vec = pl.BlockSpec((1, n_ch), lambda i: (0, 0))
    return pl.pallas_call(
        body, grid=(n_b // bt,),
        in_specs=[pl.BlockSpec((bt, n_ch), lambda i: (i, 0)),
                  pl.BlockSpec((bt, CONV_WIDTH - 1, n_ch), lambda i: (i, 0, 0)),
                  pl.BlockSpec((CONV_WIDTH, n_ch), lambda i: (0, 0)), vec, vec, vec],
        out_specs=pl.BlockSpec((bt, n_ch), lambda i: (i, 0)),
        out_shape=jax.ShapeDtypeStruct((n_b, n_ch), BF16),
        compiler_params=_params(("parallel",)), name=name)(u, buf, w_dw, b_dw, g_n, b_n)


def _sgu_gate_prompt(uv, w_s, b_s_t, g_v, b_v, *, n_batch, seq, name):
    n_chunks = seq // CHUNK
    gd = SGU_GROUP_DIM

    def body(u_ref, v_ref, ws_ref, bs_ref, g_ref, b_ref, o_ref, vo_ref):
        vn = _ln_rows(v_ref[...], g_ref[...], b_ref[...])
        vo_ref[...] = vn
        vb = vn.astype(BF16)
        row = lax.broadcasted_iota(jnp.int32, (CHUNK, CHUNK), 0)
        col = lax.broadcasted_iota(jnp.int32, (CHUNK, CHUNK), 1)
        keep = col <= row
        for grp in range(N_SGU_GROUPS):
            cols = slice(grp * gd, (grp + 1) * gd)
            wm = jnp.where(keep, ws_ref[grp], 0.0).astype(BF16)
            gate = jnp.dot(wm, vb[:, cols], preferred_element_type=F32) + bs_ref[:, grp:grp + 1]
            o_ref[:, cols] = (u_ref[:, cols] * gate).astype(o_ref.dtype)

    nb = D_SGU // D_SGU
    in_specs = [
        pl.BlockSpec((CHUNK, D_SGU), lambda b, n: (b * n_chunks + n, 0)),
        pl.BlockSpec((CHUNK, D_SGU), lambda b, n: (b * n_chunks + n, nb)),
        pl.BlockSpec((N_SGU_GROUPS, CHUNK, CHUNK), lambda b, n: (0, 0, 0)),
        pl.BlockSpec((CHUNK, N_SGU_GROUPS), lambda b, n: (0, 0)),
        pl.BlockSpec((1, D_SGU), lambda b, n: (0, 0)),
        pl.BlockSpec((1, D_SGU), lambda b, n: (0, 0)),
    ]
    out_specs = [pl.BlockSpec((CHUNK, D_SGU), lambda b, n: (b * n_chunks + n, 0)),
                 pl.BlockSpec((None, CHUNK, D_SGU), lambda b, n: (b, 0, 0))]
    out_shape = [jax.ShapeDtypeStruct((n_batch * seq, D_SGU), BF16),
                 jax.ShapeDtypeStruct((n_batch, CHUNK, D_SGU), F32)]
    return pl.pallas_call(body, grid=(n_batch, n_chunks), in_specs=in_specs, out_specs=out_specs,
                          out_shape=out_shape, compiler_params=_params(("parallel", "arbitrary")),
                          name=name)(uv, uv, w_s, b_s_t, g_v, b_v)


def _sgu_gate_decode(uv, w_vec, b_vec, g_v, b_v, *, name):
    n_b = uv.shape[0]

    def body(u_ref, v_ref, w_ref, bs_ref, g_ref, b_ref, o_ref, vo_ref):
        vn = _ln_rows(v_ref[...], g_ref[...], b_ref[...])
        vo_ref[...] = vn
        o_ref[...] = (u_ref[...] * (vn * w_ref[...] + bs_ref[...])).astype(o_ref.dtype)

    vec = pl.BlockSpec((1, D_SGU), lambda i: (0, 0))
    return pl.pallas_call(
        body, grid=(1,),
        in_specs=[pl.BlockSpec((n_b, D_SGU), lambda i: (0, 0)), pl.BlockSpec((n_b, D_SGU), lambda i: (0, 1)),
                  vec, vec, vec, vec],
        out_specs=[pl.BlockSpec((n_b, D_SGU), lambda i: (0, 0)), pl.BlockSpec((n_b, D_SGU), lambda i: (0, 0))],
        out_shape=[jax.ShapeDtypeStruct((n_b, D_SGU), BF16), jax.ShapeDtypeStruct((n_b, D_SGU), F32)],
        compiler_params=_params(("arbitrary",)), name=name)(uv, uv, w_vec, b_vec, g_v, b_v)


def _router(x, mod, w_r, b_r, l, *, tm, name):
    m_rows = x.shape[0]
    marr, rows_per_b = mod
    r = marr.shape[1]
    n_e = N_EXPERTS

    def body(x_ref, sh_ref, sc_ref, w_ref, b_ref, h_ref, gates_ref, idx_ref):
        h = x_ref[...] * (1.0 + sc_ref[...]) + sh_ref[...]
        h_ref[...] = h
        logits = jnp.dot(h, w_ref[...], preferred_element_type=F32,
                         precision=lax.Precision.HIGHEST) + b_ref[...]
        lane = lax.broadcasted_iota(jnp.int32, logits.shape, 1)
        m1 = jnp.max(logits, axis=-1, keepdims=True)
        i1 = jnp.min(jnp.where(logits == m1, lane, n_e), axis=-1, keepdims=True)
        rest = jnp.where(lane == i1, -jnp.inf, logits)
        m2 = jnp.max(rest, axis=-1, keepdims=True)
        i2 = jnp.min(jnp.where(rest == m2, lane, n_e), axis=-1, keepdims=True)
        e2 = jnp.exp(m2 - m1)
        den = 1.0 + e2
        gates_ref[...] = jnp.where(lane == i1, 1.0 / den, 0.0) + jnp.where(lane == i2, e2 / den, 0.0)
        idx_ref[...] = jnp.where(lane == 0, i1, jnp.where(lane == 1, i2, 0))

    in_specs = [
        pl.BlockSpec((tm, D_MODEL), lambda i: (i, 0)),
        pl.BlockSpec((None, r, D_MODEL), lambda i: ((i * tm) // rows_per_b, 0, 3)),
        pl.BlockSpec((None, r, D_MODEL), lambda i: ((i * tm) // rows_per_b, 0, 4)),
        pl.BlockSpec((None, D_MODEL, n_e), lambda i: (l, 0, 0)),
        pl.BlockSpec((None, 1, n_e), lambda i: (l, 0, 0)),
    ]
    out_specs = [pl.BlockSpec((tm, D_MODEL), lambda i: (i, 0)),
                 pl.BlockSpec((tm, n_e), lambda i: (i, 0)),
                 pl.BlockSpec((tm, n_e), lambda i: (i, 0))]
    out_shape = [jax.ShapeDtypeStruct((m_rows, D_MODEL), F32),
                 jax.ShapeDtypeStruct((m_rows, n_e), F32),
                 jax.ShapeDtypeStruct((m_rows, n_e), jnp.int32)]
    return pl.pallas_call(body, grid=(m_rows // tm,), in_specs=in_specs, out_specs=out_specs,
                          out_shape=out_shape, compiler_params=_params(("parallel",)), name=name)(
        x, marr, marr, w_r, b_r)


def _gather_rows(h, row_src, n_used, *, tm, name):
    n_rows = row_src.shape[0]
    d = h.shape[1]

    def body(rs_ref, nu_ref, h_ref, o_ref, buf_ref, sem):
        i = pl.program_id(0)

        @pl.when(i < nu_ref[0])
        def _():
            base = i * tm

            def issue(r, c):
                pltpu.make_async_copy(h_ref.at[pl.ds(rs_ref[base + r], 1)], buf_ref.at[pl.ds(r, 1)], sem).start()
                return c

            lax.fori_loop(0, tm, issue, 0, unroll=8)

            def wait(r, c):
                pltpu.make_async_copy(h_ref.at[pl.ds(0, 1)], buf_ref.at[pl.ds(r, 1)], sem).wait()
                return c

            lax.fori_loop(0, tm, wait, 0, unroll=8)
            o_ref[...] = buf_ref[...].astype(o_ref.dtype)

        @pl.when(i >= nu_ref[0])
        def _():
            o_ref[...] = jnp.zeros(o_ref.shape, o_ref.dtype)

    grid_spec = pltpu.PrefetchScalarGridSpec(
        num_scalar_prefetch=2, grid=(n_rows // tm,), in_specs=[pl.BlockSpec(memory_space=pl.ANY)],
        out_specs=pl.BlockSpec((tm, d), lambda i, rs, nu: (i, 0)),
        scratch_shapes=[pltpu.VMEM((tm, d), h.dtype), pltpu.SemaphoreType.DMA(())])
    return pl.pallas_call(body, grid_spec=grid_spec, out_shape=jax.ShapeDtypeStruct((n_rows, d), BF16),
                          compiler_params=_params(("arbitrary",)), name=name)(row_src, n_used, h)


def _moe_combine(expert_out, pos, wts, x, mod, ln_g, ln_b, ln_idx, *, tm, name):
    m_rows = x.shape[0]
    marr, rows_per_b = mod
    r = marr.shape[1]

    def body(pos_ref, eo_ref, w_ref, x_ref, gt_ref, g_ref, b_ref, o_ref, buf_ref, sem):
        base = pl.program_id(0) * tm

        def issue(t, c):
            for k in range(TOP_K):
                src = pos_ref[(base + t) * TOP_K + k]
                pltpu.make_async_copy(eo_ref.at[pl.ds(src, 1)], buf_ref.at[k, pl.ds(t, 1)], sem).start()
            return c

        lax.fori_loop(0, tm, issue, 0, unroll=4)

        def wait(t, c):
            for k in range(TOP_K):
                pltpu.make_async_copy(eo_ref.at[pl.ds(0, 1)], buf_ref.at[k, pl.ds(t, 1)], sem).wait()
            return c

        lax.fori_loop(0, tm, wait, 0, unroll=4)
        w = w_ref[...]
        f = w[:, 0:1] * buf_ref[0] + w[:, 1:2] * buf_ref[1]
        o_ref[...] = _ln_rows(ALPHA * x_ref[...] + (1.0 + gt_ref[...]) * f, g_ref[...], b_ref[...])

    in_specs = [
        pl.BlockSpec(memory_space=pl.ANY),
        pl.BlockSpec((tm, TOP_K), lambda i, p: (i, 0)),
        pl.BlockSpec((tm, D_MODEL), lambda i, p: (i, 0)),
        pl.BlockSpec((None, r, D_MODEL), lambda i, p: ((i * tm) // rows_per_b, 0, 5)),
        pl.BlockSpec((None, 1, D_MODEL), lambda i, p: (ln_idx, 0, 0)),
        pl.BlockSpec((None, 1, D_MODEL), lambda i, p: (ln_idx, 0, 0)),
    ]
    grid_spec = pltpu.PrefetchScalarGridSpec(
        num_scalar_prefetch=1, grid=(m_rows // tm,), in_specs=in_specs,
        out_specs=pl.BlockSpec((tm, D_MODEL), lambda i, p: (i, 0)),
        scratch_shapes=[pltpu.VMEM((TOP_K, tm, D_MODEL), F32), pltpu.SemaphoreType.DMA(())])
    return pl.pallas_call(body, grid_spec=grid_spec, out_shape=jax.ShapeDtypeStruct(x.shape, F32),
                          compiler_params=_params(("parallel",)), name=name)(
        pos, expert_out, wts, x, marr, ln_g, ln_b)


def _route_tables(idx, n_tokens, tm):
    n_pairs = n_tokens * TOP_K
    n_tiles = -(-(n_pairs + N_EXPERTS * (tm - 1)) // tm)
    e_flat = idx[:, :TOP_K].reshape(-1)
    onehot = (e_flat[:, None] == jnp.arange(N_EXPERTS, dtype=jnp.int32)[None, :]).astype(jnp.int32)
    rank = jnp.take_along_axis(jnp.cumsum(onehot, axis=0), e_flat[:, None], axis=1)[:, 0] - 1
    counts = jnp.sum(onehot, axis=0)
    tiles_per = (counts + tm - 1) // tm
    tile_end = jnp.cumsum(tiles_per)
    tile_start = tile_end - tiles_per
    pos = (tile_start[e_flat] * tm + rank).astype(jnp.int32)
    row_src = jnp.zeros((n_tiles * tm,), jnp.int32).at[pos].set(
        jnp.arange(n_pairs, dtype=jnp.int32) // TOP_K)
    n_used = tile_end[-1].astype(jnp.int32)
    tile_ids = jnp.arange(n_tiles, dtype=jnp.int32)
    tile_e = jnp.sum((tile_ids[:, None] >= tile_end[None, :]).astype(jnp.int32), axis=1)
    last_e = jnp.sum((n_used - 1 >= tile_end).astype(jnp.int32))
    tile_e = jnp.where(tile_ids < n_used, tile_e, last_e).astype(jnp.int32)
    return row_src, tile_e, n_used.reshape(1), pos


def _rope_tables(pos, width_blocks):
    half = QK_ROPE_DIM // 2
    inv = ROPE_THETA ** (-jnp.arange(half, dtype=F32) / half)
    ang = pos.astype(F32)[:, None] * inv[None, :]
    zeros = jnp.zeros((pos.shape[0], LANES - QK_ROPE_DIM), F32)
    ta = jnp.concatenate([jnp.cos(ang), jnp.cos(ang), zeros], axis=1)
    tb = jnp.concatenate([jnp.sin(ang), jnp.sin(ang), zeros], axis=1)
    return jnp.tile(ta, (1, width_blocks)), jnp.tile(tb, (1, width_blocks))


def _rot_cols(w):
    half = QK_ROPE_DIM // 2
    return jnp.concatenate([-w[..., half:], w[..., :half]], axis=-1)


def _pad_lanes(w):
    return jnp.concatenate([w, jnp.zeros(w.shape[:-1] + (LANES - w.shape[-1],), w.dtype)], axis=-1)


def kernel(x_prompt, x_sample, cache_latent, cache_krope, state_conv, page_table, c_prompt, c_sample,
           w_ada, b_ada, ln_g, ln_b,
           mla_w_dqkv, mla_g_q, mla_g_kv, mla_w_uq, mla_w_qr, mla_w_uk, mla_w_uv, mla_w_o,
           conv_w_pw1, conv_b_pw1, conv_w_dw, conv_b_dw, conv_g_n, conv_b_n, conv_w_pw2, conv_b_pw2,
           sgu_w_in, sgu_b_in, sgu_g_v, sgu_b_v, sgu_w_s, sgu_b_s, sgu_w_out, sgu_b_out,
           ffn_w_gu, ffn_w_down, moe_w_r, moe_b_r, moe_w_gu, moe_w_down):
    n_batch, seq, d = x_prompt.shape
    n_dec = x_sample.shape[0]
    assert x_sample.shape[1] == 1 and d == D_MODEL
    past = page_table.shape[1] * cache_latent.shape[2]
    t_p = n_batch * seq
    tm_p = min(512, seq)
    tm_ff = min(1024, seq)
    tn_up = 1024
    tm_moe = min(512, seq)

    n_c = n_batch + n_dec
    n_c_pad = -(-n_c // 16) * 16
    c_all = jnp.concatenate([c_prompt, c_sample, jnp.zeros((n_c_pad - n_c, d), F32)], axis=0)
    b_ada3 = b_ada.reshape(DEPTH, 1, N_MOD * d)
    mods_p, mods_s = [], []
    for l in range(DEPTH):
        tn = 1024
        (mod_l,) = _proj(
            c_all, [(w_ada, (l,), 0)], lambda ys, es: ys,
            [((n_c_pad, N_MOD * d), F32, (n_c_pad, tn), lambda i, j: (0, j))],
            tm=n_c_pad, tn=tn, nj=N_MOD * d // tn, bs=[(b_ada3, l, 0)], silu_in=True, name=f"ada{l}")
        mods_p.append(mod_l[:n_batch].reshape(n_batch, 1, N_MOD * d))
        mods_s.append(mod_l[n_batch:n_c].reshape(1, n_dec, N_MOD * d))

    ln_g3 = ln_g.reshape(DEPTH * 2, 1, d)
    ln_b3 = ln_b.reshape(DEPTH * 2, 1, d)

    n_a = mla_w_dqkv.shape[0]
    hq = N_HEADS * QK_NOPE_DIM
    w_uq2 = mla_w_uq.reshape(n_a, Q_LORA_RANK, hq)
    w_uk2 = mla_w_uk.reshape(n_a, KV_LORA_RANK, hq)
    w_uv2 = mla_w_uv.reshape(n_a, KV_LORA_RANK, N_HEADS * V_HEAD_DIM)
    w_qr_a = _pad_lanes(mla_w_qr).reshape(n_a, Q_LORA_RANK, N_HEADS * LANES)
    w_qr_b = _pad_lanes(_rot_cols(mla_w_qr)).reshape(n_a, Q_LORA_RANK, N_HEADS * LANES)
    w_kr = mla_w_dqkv[:, :, Q_LORA_RANK + KV_LORA_RANK:]
    w_kr_a = _pad_lanes(w_kr)
    w_kr_b = _pad_lanes(_rot_cols(w_kr))
    g_q3 = mla_g_q.reshape(n_a, 1, Q_LORA_RANK)
    g_kv3 = mla_g_kv.reshape(n_a, 1, KV_LORA_RANK)
    n_ff = ffn_w_gu.shape[0]
    n_moe = moe_w_r.shape[0]
    b_r3 = moe_b_r.reshape(n_moe, 1, N_EXPERTS)

    def mla_front(x, mod, rows_per_b, tm_down, tm, tabs, tabs_wide, j, tag):
        ta, tb = tabs
        c_q, c_kv, kr = _mla_down(x, (mod, rows_per_b), mla_w_dqkv, j, w_kr_a[j], w_kr_b[j],
                                  g_q3, g_kv3, ta, tb, tm=tm_down, name=f"mla_down_{tag}")
        m_rows = x.shape[0]
        tn = tn_up
        n_tab = ta.shape[0] // tm
        (qn,) = _proj(c_q, [(w_uq2, (j,), 0)], lambda ys, es: ys,
                      [((m_rows, hq), BF16 if tag[0] == 'p' else F32, (tm, tn), lambda i, jj: (i, jj))],
                      tm=tm, tn=tn, nj=hq // tn, name=f"mla_qn_{tag}")
        taw, tbw = tabs_wide
        (qr,) = _proj(c_q, [(w_qr_a, (j,), 0), (w_qr_b, (j,), 0)],
                      lambda ys, es: [ys[0] * es[0] + ys[1] * es[1]],
                      [((m_rows, hq), BF16 if tag[0] == 'p' else F32, (tm, tn), lambda i, jj: (i, jj))],
                      tm=tm, tn=tn, nj=hq // tn,
                      extras=[(taw, (tm, tn), lambda i, jj: (i % n_tab, 0)),
                              (tbw, (tm, tn), lambda i, jj: (i % n_tab, 0))],
                      name=f"mla_qr_{tag}")
        return c_q, c_kv, kr, qn, qr

    def res_args(x, mod, comp, rows_per_b, l, s):
        return (x, mod, comp, rows_per_b, ln_g3, ln_b3, 2 * l + s)

    def dense_ffn(x, mod, rows_per_b, tm_a, tm_b, l, tag):
        jf = l // 2
        tn = 256
        nb = DENSE_FF // tn
        (act,) = _proj(x, [(ffn_w_gu, (jf,), 0), (ffn_w_gu, (jf,), nb)],
                       lambda ys, es: [ys[0] * jax.nn.sigmoid(ys[0]) * ys[1]],
                       [((x.shape[0], DENSE_FF), BF16, (tm_a, tn), lambda i, jj: (i, jj))],
                       tm=tm_a, tn=tn, nj=nb, mod=(mod, 3, 4, rows_per_b), name=f"ffn_gu_{tag}")
        return _proj_k(act, ffn_w_down, (jf,), tm=tm_b, tk=512,
                       res=res_args(x, mod, 5, rows_per_b, l, 1), name=f"ffn_down_{tag}")

    pos_p = jnp.arange(seq, dtype=jnp.int32)
    pos_s = jnp.full((n_dec,), past, dtype=jnp.int32)
    tabs_p, tabs_s = _rope_tables(pos_p, 1), _rope_tables(pos_s, 1)
    tabs_pw, tabs_sw = _rope_tables(pos_p, tn_up // LANES), _rope_tables(pos_s, tn_up // LANES)

    xp = x_prompt.reshape(t_p, d)
    xs = x_sample.reshape(n_dec, d)
    cache_kr_t = jnp.swapaxes(cache_krope, 2, 3)
    lat_p, kr_p, lat_s, kr_s = [], [], [], []
    conv_p = conv_s = v_p = v_s = None

    for l in range(DEPTH):
        kind, j = l % 3, l // 3
        mp, ms = mods_p[l], mods_s[l]
        if kind == 0:
            c_q, c_kv, kr, qn, qr = mla_front(xp, mp, seq, tm_p, tm_ff, tabs_p, tabs_pw, j, f"p{l}")
            tn = tn_up
            (kn,) = _proj(c_kv, [(w_uk2, (j,), 0)], lambda ys, es: ys,
                          [((t_p, hq), BF16, (tm_ff, tn), lambda i, jj: (i, jj))],
                          tm=tm_ff, tn=tn, nj=hq // tn, name=f"mla_kn_p{l}")
            (vv,) = _proj(c_kv, [(w_uv2, (j,), 0)], lambda ys, es: ys,
                          [((t_p, hq), BF16, (tm_ff, tn), lambda i, jj: (i, jj))],
                          tm=tm_ff, tn=tn, nj=hq // tn, name=f"mla_v_p{l}")
            o = _attn_prompt(qn, qr, kn, kr, vv, n_batch=n_batch, seq=seq, tq=min(512, seq),
                             name=f"attn_p{l}")
            xp = _proj_k(o, mla_w_o, (j,), tm=tm_p, tk=1024, res=res_args(xp, mp, 2, seq, l, 0),
                         name=f"mla_o_p{l}")
            lat_p.append(c_kv.reshape(n_batch, seq, KV_LORA_RANK))
            kr_p.append(kr[:, :QK_ROPE_DIM].reshape(n_batch, seq, QK_ROPE_DIM))
            c_q, c_kv, kr, qn, qr = mla_front(xs, ms, n_dec, n_dec, n_dec, tabs_s, tabs_sw, j, f"s{l}")
            q_lat = _head_mix(qn, w_uk2, j, contract_w_last=True, name=f"mla_qlat_s{l}")
            o_lat = _attn_decode(q_lat.reshape(n_dec, N_HEADS, KV_LORA_RANK),
                                 qr.reshape(n_dec, N_HEADS, LANES),
                                 c_kv.reshape(n_dec, 1, KV_LORA_RANK), kr.reshape(n_dec, 1, LANES),
                                 cache_latent, cache_kr_t, page_table, j, name=f"attn_s{l}")
            o = _head_mix(o_lat.reshape(n_dec, N_HEADS * KV_LORA_RANK), w_uv2, j,
                          contract_w_last=False, name=f"mla_ov_s{l}")
            xs = _proj_k(o, mla_w_o, (j,), tm=n_dec, tk=1024, res=res_args(xs, ms, 2, n_dec, l, 0),
                         name=f"mla_o_s{l}")
            lat_s.append(c_kv.reshape(n_dec, 1, KV_LORA_RANK))
            kr_s.append(kr[:, :QK_ROPE_DIM].reshape(n_dec, 1, QK_ROPE_DIM))
        elif kind == 1:
            d_c = conv_w_pw1.shape[2] // 2
            b1 = conv_b_pw1.reshape(-1, 1, 2 * d_c)
            b2 = conv_b_pw2.reshape(-1, 1, d)
            tn = 256
            glu = lambda ys, es: [ys[0] * jax.nn.sigmoid(ys[1])]

            def pw1(x, mod, rows_per_b, tm, tag):
                return _proj(x, [(conv_w_pw1, (j,), 0), (conv_w_pw1, (j,), d_c // tn)], glu,
                             [((x.shape[0], d_c), F32, (tm, tn), lambda i, jj: (i, jj))],
                             tm=tm, tn=tn, nj=d_c // tn, bs=[(b1, j, 0), (b1, j, d_c // tn)],
                             mod=(mod, 0, 1, rows_per_b), name=f"conv_pw1_{tag}")[0]

            u = pw1(xp, mp, seq, tm_ff, f"p{l}")
            y = _dwconv_prompt(u, conv_w_dw[j], conv_b_dw[j].reshape(1, d_c), conv_g_n[j].reshape(1, d_c),
                               conv_b_n[j].reshape(1, d_c), n_batch=n_batch, seq=seq, tm=tm_p,
                               name=f"dwconv_p{l}")
            xp = _proj_k(y, conv_w_pw2, (j,), tm=tm_p, tk=1024, bias=(b2, j),
                         res=res_args(xp, mp, 2, seq, l, 0), name=f"conv_pw2_p{l}")
            conv_p = u.reshape(n_batch, seq, d_c)[:, seq - (CONV_WIDTH - 1):][None]

            u = pw1(xs, ms, n_dec, n_dec, f"s{l}")
            y = _dwconv_decode(u, state_conv[j], conv_w_dw[j], conv_b_dw[j].reshape(1, d_c),
                               conv_g_n[j].reshape(1, d_c), conv_b_n[j].reshape(1, d_c), name=f"dwconv_s{l}")
            xs = _proj_k(y, conv_w_pw2, (j,), tm=n_dec, tk=1024, bias=(b2, j),
                         res=res_args(xs, ms, 2, n_dec, l, 0), name=f"conv_pw2_s{l}")
            conv_s = jnp.concatenate([state_conv[j][:, 1:], u[:, None, :]], axis=1)[None]
        else:
            b_in = sgu_b_in.reshape(-1, 1, 2 * D_SGU)
            b_out = sgu_b_out.reshape(-1, 1, d)
            tn = 512
            gelu = lambda ys, es: [_gelu_tanh(ys[0])]

            def sgu_in(x, mod, rows_per_b, tm, tag):
                return _proj(x, [(sgu_w_in, (j,), 0)], gelu,
                             [((x.shape[0], 2 * D_SGU), F32, (tm, tn), lambda i, jj: (i, jj))],
                             tm=tm, tn=tn, nj=2 * D_SGU // tn, bs=[(b_in, j, 0)],
                             mod=(mod, 0, 1, rows_per_b), name=f"sgu_in_{tag}")[0]

            g_v = sgu_g_v[j].reshape(1, D_SGU)
            b_v = sgu_b_v[j].reshape(1, D_SGU)
            uv = sgu_in(xp, mp, seq, tm_ff, f"p{l}")
            ug, v_last = _sgu_gate_prompt(uv, sgu_w_s[j], sgu_b_s[j].T, g_v, b_v, n_batch=n_batch, seq=seq,
                                          name=f"sgu_gate_p{l}")
            xp = _proj_k(ug, sgu_w_out, (j,), tm=tm_p, tk=1024, bias=(b_out, j),
                         res=res_args(xp, mp, 2, seq, l, 0), name=f"sgu_out_p{l}")
            v_p = v_last[None]

            uv = sgu_in(xs, ms, n_dec, n_dec, f"s{l}")
            w_vec = jnp.repeat(sgu_w_s[j][:, 0, 0], SGU_GROUP_DIM).reshape(1, D_SGU)
            b_vec = jnp.repeat(sgu_b_s[j][:, 0], SGU_GROUP_DIM).reshape(1, D_SGU)
            ug, vn = _sgu_gate_decode(uv, w_vec, b_vec, g_v, b_v, name=f"sgu_gate_s{l}")
            xs = _proj_k(ug, sgu_w_out, (j,), tm=n_dec, tk=1024, bias=(b_out, j),
                         res=res_args(xs, ms, 2, n_dec, l, 0), name=f"sgu_out_s{l}")
            v_s = vn.reshape(n_dec, 1, D_SGU)[None]

        if l % 2 == 0:
            xp = dense_ffn(xp, mp, seq, tm_ff, tm_p, l, f"p{l}")
            xs = dense_ffn(xs, ms, n_dec, n_dec, n_dec, l, f"s{l}")
        else:
            jm = l // 2
            h_p, gates_p, idx_p = _router(xp, (mp, seq), moe_w_r, b_r3, jm, tm=tm_p, name=f"router_p{l}")
            h_s, gates_s, idx_s = _router(xs, (ms, n_dec), moe_w_r, b_r3, jm, tm=n_dec, name=f"router_s{l}")
            h_all = jnp.concatenate([h_p, h_s], axis=0)
            idx_all = jnp.concatenate([idx_p, idx_s], axis=0)
            gates_all = jnp.concatenate([gates_p, gates_s], axis=0)
            n_tok = t_p + n_dec
            row_src, tile_e, n_used, pos = _route_tables(idx_all, n_tok, tm_moe)
            wts = jnp.take_along_axis(gates_all, idx_all[:, :TOP_K], axis=1)
            n_rows = row_src.shape[0]
            tn = 512
            nb = EXPERT_FF // tn
            h_sorted = _gather_rows(h_all, row_src, n_used, tm=tm_moe, name=f"moe_gather{l}")
            (act,) = _proj(h_sorted, [(moe_w_gu, (jm, 'e'), 0), (moe_w_gu, (jm, 'e'), nb)],
                           lambda ys, es: [ys[0] * jax.nn.sigmoid(ys[0]) * ys[1]],
                           [((n_rows, EXPERT_FF), BF16, (tm_moe, tn), lambda i, jj: (i, jj))],
                           tm=tm_moe, tn=tn, nj=nb, expert=(tile_e, n_used), cols_outer=True,
                           name=f"moe_gu{l}")
            eo = _proj_k(act, moe_w_down, (jm, 'e'), tm=tm_moe, tk=1024, expert=(tile_e, n_used),
                         name=f"moe_down{l}")
            xp = _moe_combine(eo, pos[:t_p * TOP_K], wts[:t_p], xp, (mp, seq), ln_g3, ln_b3, 2 * l + 1,
                              tm=tm_p, name=f"moe_comb_p{l}")
            xs = _moe_combine(eo, pos[t_p * TOP_K:], wts[t_p:], xs, (ms, n_dec), ln_g3, ln_b3, 2 * l + 1,
                              tm=n_dec, name=f"moe_comb_s{l}")

    return (xp.reshape(n_batch, seq, d), xs.reshape(n_dec, 1, d),
            jnp.stack(lat_p), jnp.stack(kr_p), conv_p, v_p,
            jnp.stack(lat_s), jnp.stack(kr_s), conv_s, v_s)
```

```python
import functools

import jax
import jax.numpy as jnp
from jax import lax
from jax.experimental import pallas as pl
from jax.experimental.pallas import tpu as pltpu

F32 = jnp.float32
BF16 = jnp.bfloat16

D_MODEL = 2048
DEPTH = 4
PAGE_SIZE = 128
N_HEADS = 16
QK_NOPE_DIM = 128
QK_ROPE_DIM = 64
V_HEAD_DIM = 128
Q_LORA_RANK = 512
KV_LORA_RANK = 512
ROPE_THETA = 10000.0
ATTN_SCALE = (QK_NOPE_DIM + QK_ROPE_DIM) ** -0.5
CONV_WIDTH = 31
CHUNK = 128
D_SGU = 2 * D_MODEL
N_SGU_GROUPS = 16
SGU_GROUP_DIM = D_SGU // N_SGU_GROUPS
DENSE_FF = 5632
N_EXPERTS = 8
TOP_K = 2
EXPERT_FF = 7168
N_MOD = 6
ALPHA = (2 * DEPTH) ** 0.25
LN_EPS = 1e-5
RMS_EPS = 1e-6

LANES = 128
V7X_VMEM_LIMIT = 56 * 1024 * 1024
CONV_HALO = 32
PAGES_PER_STEP = 16
ROW_TILE_GROUP = 4
SEQS_PER_STEP = 2


def _params(sem):
    return pltpu.CompilerParams(dimension_semantics=sem, vmem_limit_bytes=V7X_VMEM_LIMIT)


def _ln_rows(v, g, b):
    mu = jnp.mean(v, axis=-1, keepdims=True)
    vc = v - mu
    var = jnp.mean(vc * vc, axis=-1, keepdims=True)
    return vc * lax.rsqrt(var + LN_EPS) * g + b


def _gelu_tanh(y):
    return 0.5 * y * (1.0 + jnp.tanh(0.7978845608028654 * (y + 0.044715 * (y * y * y))))


def _proj(x, ws, epi, outs, *, tm, tn, nj, bs=None, mod=None, extras=(), silu_in=False,
          expert=None, cols_outer=False, name):
    if bs is None:
        bs = [None] * len(ws)
    m_rows, k_dim = x.shape
    ni = m_rows // tm
    assert ni * tm == m_rows
    has_mod = mod is not None
    prefetch = list(expert) if expert is not None else []
    n_pf = len(prefetch)
    direct = (x.dtype == BF16 and not has_mod and not silu_in)
    assert direct or not cols_outer

    def order(f):
        if cols_outer:
            return lambda j, i, *pf: f(i, j, *pf)
        return f

    def x_imap(i, j, *pf):
        if expert is not None:
            i = jnp.minimum(i, pf[1][0] - 1)
        return (i, 0)

    def w_imap(lead, off):
        def f(i, j, *pf):
            jj = j
            if expert is not None and not cols_outer:
                jj = jnp.where(i < pf[1][0], j, nj - 1)
            return tuple(pf[0][i] if s == 'e' else s for s in lead) + (0, off + jj)
        return f

    in_specs = [pl.BlockSpec((tm, k_dim), order(x_imap))]
    args = [x]
    if has_mod:
        marr, c_sh, c_sc, rows_per_b = mod
        r = marr.shape[1]
        for comp in (c_sh, c_sc):
            in_specs.append(pl.BlockSpec(
                (None, r, k_dim), order(lambda i, j, *pf, comp=comp: ((i * tm) // rows_per_b, 0, comp))))
            args.append(marr)
    for arr, lead, off in ws:
        in_specs.append(pl.BlockSpec((None,) * len(lead) + (k_dim, tn), order(w_imap(lead, off))))
        args.append(arr)
    for b in bs:
        if b is not None:
            arr, lead, off = b
            in_specs.append(pl.BlockSpec(
                (None, 1, tn), order(lambda i, j, *pf, lead=lead, off=off: (lead, 0, off + j))))
            args.append(arr)
    for arr, bshape, imap in extras:
        in_specs.append(pl.BlockSpec(bshape, order(lambda i, j, *pf, imap=imap: imap(i, j))))
        args.append(arr)
    out_specs = [pl.BlockSpec(bshape, order(lambda i, j, *pf, imap=imap: imap(i, j)))
                 for (_, _, bshape, imap) in outs]
    out_shape = [jax.ShapeDtypeStruct(shape, dtype) for (shape, dtype, _, _) in outs]
    scratch = [] if direct else [pltpu.VMEM((tm, k_dim), BF16)]

    def body(*refs):
        it = iter(refs)
        pf = [next(it) for _ in range(n_pf)]
        x_ref = next(it)
        sh_ref = sc_ref = None
        if has_mod:
            sh_ref, sc_ref = next(it), next(it)
        w_refs = [next(it) for _ in ws]
        b_refs = [next(it) if b is not None else None for b in bs]
        e_refs = [next(it) for _ in extras]
        o_refs = [next(it) for _ in outs]
        h_ref = None if direct else next(it)
        i, j = pl.program_id(0), pl.program_id(1)
        if cols_outer:
            i, j = j, i

        if not direct:
            @pl.when(j == 0)
            def _():
                h = x_ref[...].astype(F32)
                if has_mod:
                    h = h * (1.0 + sc_ref[...]) + sh_ref[...]
                if silu_in:
                    h = h * jax.nn.sigmoid(h)
                h_ref[...] = h.astype(BF16)

        def compute():
            h = x_ref[...] if direct else h_ref[...]
            ys = []
            for w_ref, b_ref in zip(w_refs, b_refs):
                y = jnp.dot(h, w_ref[...].astype(BF16), preferred_element_type=F32)
                if b_ref is not None:
                    y = y + b_ref[...]
                ys.append(y)
            res = epi(ys, [e[...] for e in e_refs])
            for o_ref, r_ in zip(o_refs, res):
                o_ref[...] = r_.astype(o_ref.dtype)

        if expert is not None:
            @pl.when(i < pf[1][0])
            def _():
                compute()

            @pl.when(i >= pf[1][0])
            def _():
                for o_ref in o_refs:
                    o_ref[...] = jnp.zeros(o_ref.shape, o_ref.dtype)
        else:
            compute()

    grid = (nj, ni) if cols_outer else (ni, nj)
    sem = ("arbitrary", "arbitrary") if cols_outer else ("parallel", "arbitrary")
    grid_spec = pltpu.PrefetchScalarGridSpec(
        num_scalar_prefetch=n_pf, grid=grid, in_specs=in_specs, out_specs=out_specs,
        scratch_shapes=scratch)
    return pl.pallas_call(body, grid_spec=grid_spec, out_shape=out_shape,
                          compiler_params=_params(sem), name=name)(*prefetch, *args)


def _proj_k(a, w, lead, *, tm, tk, group=1, res=None, bias=None, expert=None, name):
    m_rows, k_dim = a.shape
    n_dim = w.shape[-1]
    ni, nk = m_rows // tm, k_dim // tk
    assert ni * tm == m_rows and nk * tk == k_dim and ni % group == 0
    prefetch = list(expert) if expert is not None else []
    n_pf = len(prefetch)

    def tile(g, t):
        return g * group + t

    def k_block(i, k, pf):
        return jnp.where(i < pf[1][0], k, nk - 1) if expert is not None else k

    def a_imap(g, k, t, *pf):
        i = tile(g, t)
        return (i, k_block(i, k, pf))

    def w_imap(g, k, t, *pf):
        i = tile(g, t)
        return tuple(pf[0][i] if s == 'e' else s for s in lead) + (k_block(i, k, pf), 0)

    def row_imap(g, k, t, *pf):
        return (jnp.where(k == nk - 1, tile(g, t), tile(g, 0)), 0)

    in_specs = [pl.BlockSpec((tm, tk), a_imap),
                pl.BlockSpec((None,) * len(lead) + (tk, n_dim), w_imap)]
    args = [a, w]
    if bias is not None:
        barr, bl = bias
        in_specs.append(pl.BlockSpec((None, 1, n_dim), lambda g, k, t, *pf: (bl, 0, 0)))
        args.append(barr)
    if res is not None:
        x_res, marr, c_gate, rows_per_b, ln_g, ln_b, ln_idx = res
        r = marr.shape[1]
        in_specs += [
            pl.BlockSpec((tm, n_dim), row_imap),
            pl.BlockSpec((None, r, n_dim),
                         lambda g, k, t, *pf: ((tile(g, t) * tm) // rows_per_b, 0, c_gate)),
            pl.BlockSpec((None, 1, n_dim), lambda g, k, t, *pf: (ln_idx, 0, 0)),
            pl.BlockSpec((None, 1, n_dim), lambda g, k, t, *pf: (ln_idx, 0, 0)),
        ]
        args += [x_res, marr, ln_g, ln_b]

    def body(*refs):
        it = iter(refs)
        pf = [next(it) for _ in range(n_pf)]
        a_ref, w_ref = next(it), next(it)
        b_ref = next(it) if bias is not None else None
        if res is not None:
            x_ref, gt_ref, g_ref, be_ref = next(it), next(it), next(it), next(it)
        o_ref, acc_ref = next(it), next(it)
        k, t = pl.program_id(1), pl.program_id(2)
        i = tile(pl.program_id(0), t)

        def compute():
            @pl.when(k == 0)
            def _():
                acc_ref[t] = jnp.zeros((tm, n_dim), F32)

            acc_ref[t] += jnp.dot(a_ref[...].astype(BF16), w_ref[...].astype(BF16),
                                  preferred_element_type=F32)

            @pl.when(k == nk - 1)
            def _():
                y = acc_ref[t]
                if b_ref is not None:
                    y = y + b_ref[...]
                if res is not None:
                    y = _ln_rows(ALPHA * x_ref[...] + (1.0 + gt_ref[...]) * y, g_ref[...], be_ref[...])
                o_ref[...] = y.astype(o_ref.dtype)

        if expert is not None:
            @pl.when(i < pf[1][0])
            def _():
                compute()

            @pl.when(jnp.logical_and(i >= pf[1][0], k == nk - 1))
            def _():
                o_ref[...] = jnp.zeros(o_ref.shape, o_ref.dtype)
        else:
            compute()

    grid_spec = pltpu.PrefetchScalarGridSpec(
        num_scalar_prefetch=n_pf, grid=(ni // group, nk, group), in_specs=in_specs,
        out_specs=pl.BlockSpec((tm, n_dim), row_imap),
        scratch_shapes=[pltpu.VMEM((group, tm, n_dim), F32)])
    return pl.pallas_call(body, grid_spec=grid_spec,
                          out_shape=jax.ShapeDtypeStruct((m_rows, n_dim), F32),
                          compiler_params=_params(("parallel", "arbitrary", "arbitrary")),
                          name=name)(*prefetch, *args)


def _mla_down(x, mod, w_dqkv, l, w_ka, w_kb, g_q, g_kv, tab_a, tab_b, *, tm, name):
    m_rows = x.shape[0]
    ni = m_rows // tm
    marr, rows_per_b = mod
    r = marr.shape[1]
    n_tab = tab_a.shape[0] // tm

    def body(x_ref, sh_ref, sc_ref, wq_ref, wkv_ref, wka_ref, wkb_ref, gq_ref, gkv_ref, ta_ref, tb_ref,
             cq_ref, ckv_ref, kr_ref):
        h = (x_ref[...] * (1.0 + sc_ref[...]) + sh_ref[...]).astype(BF16)

        def rms(w_ref, g_ref):
            z = jnp.dot(h, w_ref[...].astype(BF16), preferred_element_type=F32)
            return z * lax.rsqrt(jnp.mean(z * z, axis=-1, keepdims=True) + RMS_EPS) * g_ref[...]

        cq_ref[...] = rms(wq_ref, gq_ref).astype(cq_ref.dtype)
        ckv_ref[...] = rms(wkv_ref, gkv_ref)
        ka = jnp.dot(h, wka_ref[...].astype(BF16), preferred_element_type=F32)
        kb = jnp.dot(h, wkb_ref[...].astype(BF16), preferred_element_type=F32)
        kr_ref[...] = ka * ta_ref[...] + kb * tb_ref[...]

    rank = Q_LORA_RANK
    in_specs = [
        pl.BlockSpec((tm, D_MODEL), lambda i: (i, 0)),
        pl.BlockSpec((None, r, D_MODEL), lambda i: ((i * tm) // rows_per_b, 0, 0)),
        pl.BlockSpec((None, r, D_MODEL), lambda i: ((i * tm) // rows_per_b, 0, 1)),
        pl.BlockSpec((None, D_MODEL, rank), lambda i: (l, 0, 0)),
        pl.BlockSpec((None, D_MODEL, rank), lambda i: (l, 0, 1)),
        pl.BlockSpec((D_MODEL, LANES), lambda i: (0, 0)),
        pl.BlockSpec((D_MODEL, LANES), lambda i: (0, 0)),
        pl.BlockSpec((None, 1, rank), lambda i: (l, 0, 0)),
        pl.BlockSpec((None, 1, rank), lambda i: (l, 0, 0)),
        pl.BlockSpec((tm, LANES), lambda i: (i % n_tab, 0)),
        pl.BlockSpec((tm, LANES), lambda i: (i % n_tab, 0)),
    ]
    out_specs = [pl.BlockSpec((tm, rank), lambda i: (i, 0)),
                 pl.BlockSpec((tm, rank), lambda i: (i, 0)),
                 pl.BlockSpec((tm, LANES), lambda i: (i, 0))]
    out_shape = [jax.ShapeDtypeStruct((m_rows, rank), BF16),
                 jax.ShapeDtypeStruct((m_rows, rank), F32),
                 jax.ShapeDtypeStruct((m_rows, LANES), F32)]
    return pl.pallas_call(body, grid=(ni,), in_specs=in_specs, out_specs=out_specs, out_shape=out_shape,
                          compiler_params=_params(("parallel",)), name=name)(
        x, marr, marr, w_dqkv, w_dqkv, w_ka, w_kb, g_q, g_kv, tab_a, tab_b)


def _attn_prompt(qn, qr, kn, kr, v, *, n_batch, seq, tq, name):
    hd = QK_NOPE_DIM
    n_q = seq // tq

    def body(qn_ref, qr_ref, kn_ref, kr_ref, v_ref, o_ref):
        k_all = jnp.concatenate([kn_ref[...], kr_ref[...].astype(BF16)], axis=1)
        for qi in range(n_q):
            rows = slice(qi * tq, (qi + 1) * tq)
            n_kv = (qi + 1) * tq
            q = jnp.concatenate([qn_ref[rows, :], qr_ref[rows, :]], axis=1)
            s = lax.dot_general(q, k_all[:n_kv], (((1,), (1,)), ((), ())),
                                preferred_element_type=F32) * ATTN_SCALE
            row = lax.broadcasted_iota(jnp.int32, (tq, n_kv), 0) + qi * tq
            col = lax.broadcasted_iota(jnp.int32, (tq, n_kv), 1)
            s = jnp.where(col <= row, s, -jnp.inf)
            m = jnp.max(s, axis=-1, keepdims=True)
            p = jnp.exp(s - m)
            den = jnp.sum(p, axis=-1, keepdims=True)
            o = jnp.dot(p.astype(BF16), v_ref[0:n_kv, :], preferred_element_type=F32) / den
            o_ref[rows, :] = o.astype(o_ref.dtype)

    hspec = pl.BlockSpec((seq, hd), lambda b, h: (b, h))
    in_specs = [hspec, hspec, hspec, pl.BlockSpec((seq, LANES), lambda b, h: (b, 0)), hspec]
    return pl.pallas_call(body, grid=(n_batch, N_HEADS), in_specs=in_specs, out_specs=hspec,
                          out_shape=jax.ShapeDtypeStruct(qn.shape, BF16),
                          compiler_params=_params(("parallel", "parallel")), name=name)(qn, qr, kn, kr, v)


def _head_mix(a, w, l, *, contract_w_last, name):
    n_b = a.shape[0]
    c_dim = w.shape[1]
    hd = QK_NOPE_DIM

    def body(a_ref, w_ref, o_ref):
        wv = w_ref[...].astype(BF16)
        av = a_ref[...].astype(BF16)
        if contract_w_last:
            o_ref[...] = lax.dot_general(av, wv, (((1,), (1,)), ((), ())), preferred_element_type=F32)
        else:
            o_ref[...] = jnp.dot(av, wv, preferred_element_type=F32)

    a_w, o_w = (hd, c_dim) if contract_w_last else (c_dim, hd)
    return pl.pallas_call(
        body, grid=(N_HEADS,),
        in_specs=[pl.BlockSpec((n_b, a_w), lambda h: (0, h)),
                  pl.BlockSpec((None, c_dim, hd), lambda h: (l, 0, h))],
        out_specs=pl.BlockSpec((n_b, o_w), lambda h: (0, h)),
        out_shape=jax.ShapeDtypeStruct((n_b, N_HEADS * o_w), F32),
        compiler_params=_params(("parallel",)), name=name)(a, w)


def _attn_decode(q_lat, q_rope, c_new, kr_new, cache_lat, cache_kr, page_table, l, *, name):
    n_b, n_pages = page_table.shape
    g, nq = PAGES_PER_STEP, SEQS_PER_STEP
    n_steps = n_pages // g
    assert n_steps * g == n_pages and n_b % nq == 0
    c_dim, r_dim = KV_LORA_RANK, QK_ROPE_DIM
    pt_flat = page_table.reshape(-1)
    n_pg = nq * g
    n_total = (n_b // nq) * n_steps

    def body(pt_ref, ql_ref, qr_ref, cn_ref, kn_ref, clat_ref, ckr_ref, o_ref,
             cbuf_ref, kbuf_ref, sems, m_ref, d_ref, acc_ref):
        s_idx = pl.program_id(1)
        step = pl.program_id(0) * n_steps + s_idx
        slot = step % 2
        nt = (((1,), (1,)), ((), ()))

        def page_copies(step_idx, slot_idx):
            b0 = (step_idx // n_steps) * nq
            s0 = (step_idx % n_steps) * g
            out = []
            for q in range(nq):
                for t in range(g):
                    page = pt_ref[(b0 + q) * n_pages + s0 + t]
                    out.append(pltpu.make_async_copy(
                        clat_ref.at[l, page], cbuf_ref.at[slot_idx, q * g + t], sems.at[slot_idx]))
                    out.append(pltpu.make_async_copy(
                        ckr_ref.at[l, page], kbuf_ref.at[slot_idx, q * g + t], sems.at[slot_idx]))
            return out

        @pl.when(step == 0)
        def _():
            for cp in page_copies(0, 0):
                cp.start()

        @pl.when(step + 1 < n_total)
        def _():
            for cp in page_copies(step + 1, 1 - slot):
                cp.start()

        for cp in page_copies(step, slot):
            cp.wait()
        c_refs = [cbuf_ref.at[slot, i] for i in range(n_pg)]
        kr_refs = [kbuf_ref.at[slot, i] for i in range(n_pg)]

        @pl.when(s_idx == 0)
        def _():
            for q in range(nq):
                cn = cn_ref[q]
                s_new = (jnp.sum(ql_ref[q] * cn, axis=-1, keepdims=True)
                         + jnp.sum(qr_ref[q][:, :r_dim] * kn_ref[q][:, :r_dim], axis=-1, keepdims=True))
                m_ref[q] = s_new * ATTN_SCALE
                d_ref[q] = jnp.ones((N_HEADS, 1), F32)
                acc_ref[q] = jnp.broadcast_to(cn, (N_HEADS, c_dim))

        for q in range(nq):
            qlb = ql_ref[q].astype(BF16)
            qrb = qr_ref[q][:, :r_dim].astype(BF16)
            cs = [c_refs[q * g + t][...].astype(BF16) for t in range(g)]
            s = jnp.concatenate(
                [lax.dot_general(qlb, cs[t], nt, preferred_element_type=F32)
                 + jnp.dot(qrb, kr_refs[q * g + t][...].astype(BF16), preferred_element_type=F32)
                 for t in range(g)], axis=1) * ATTN_SCALE
            m_old = m_ref[q]
            m_new = jnp.maximum(m_old, jnp.max(s, axis=-1, keepdims=True))
            scale_old = jnp.exp(m_old - m_new)
            p = jnp.exp(s - m_new)
            d_ref[q] = d_ref[q] * scale_old + jnp.sum(p, axis=-1, keepdims=True)
            pb = p.astype(BF16)
            pv = jnp.dot(pb[:, 0:PAGE_SIZE], cs[0], preferred_element_type=F32)
            for t in range(1, g):
                pv = pv + jnp.dot(pb[:, t * PAGE_SIZE:(t + 1) * PAGE_SIZE], cs[t],
                                  preferred_element_type=F32)
            acc_ref[q] = acc_ref[q] * scale_old + pv
            m_ref[q] = m_new

        @pl.when(s_idx == n_steps - 1)
        def _():
            o_ref[...] = acc_ref[...] / d_ref[...]

    def seq_spec(rows, width):
        return pl.BlockSpec((nq, rows, width), lambda b, s, pt: (b, 0, 0))

    in_specs = [seq_spec(N_HEADS, c_dim), seq_spec(N_HEADS, LANES), seq_spec(1, c_dim), seq_spec(1, LANES),
                pl.BlockSpec(memory_space=pl.ANY), pl.BlockSpec(memory_space=pl.ANY)]
    grid_spec = pltpu.PrefetchScalarGridSpec(
        num_scalar_prefetch=1, grid=(n_b // nq, n_steps), in_specs=in_specs,
        out_specs=seq_spec(N_HEADS, c_dim),
        scratch_shapes=[pltpu.VMEM((2, n_pg, PAGE_SIZE, c_dim), F32),
                        pltpu.VMEM((2, n_pg, r_dim, PAGE_SIZE), F32),
                        pltpu.SemaphoreType.DMA((2,)),
                        pltpu.VMEM((nq, N_HEADS, 1), F32), pltpu.VMEM((nq, N_HEADS, 1), F32),
                        pltpu.VMEM((nq, N_HEADS, c_dim), F32)])
    return pl.pallas_call(body, grid_spec=grid_spec,
                          out_shape=jax.ShapeDtypeStruct((n_b, N_HEADS, c_dim), F32),
                          compiler_params=_params(("arbitrary", "arbitrary")), name=name)(
        pt_flat, q_lat, q_rope, c_new, kr_new, cache_lat, cache_kr)


def _dwconv_prompt(u, w_dw, b_dw, g_n, b_n, *, n_batch, seq, tm, name):
    n_ch = u.shape[1]
    n_cc = n_ch // LANES
    nt = seq // tm
    halo_blocks = tm // CONV_HALO
    rc = 64
    first = CONV_HALO - (CONV_WIDTH - 1)

    def body(u_ref, halo_ref, w_ref, bd_ref, g_ref, b_ref, o_ref, ext_ref, y_ref):
        t, c = pl.program_id(1), pl.program_id(2)
        ext_ref[0:CONV_HALO, :] = jnp.where(t == 0, 0.0, halo_ref[...])
        ext_ref[CONV_HALO:, :] = u_ref[...]
        taps = [w_ref[k:k + 1, :] for k in range(CONV_WIDTH)]
        for r0 in range(0, tm, rc):
            acc = jnp.zeros((rc, LANES), F32)
            for k in range(CONV_WIDTH):
                acc = acc + taps[k] * ext_ref[r0 + first + k:r0 + first + k + rc, :]
            y_ref[c, r0:r0 + rc, :] = acc + bd_ref[...]

        @pl.when(c == n_cc - 1)
        def _():
            y = y_ref[...]
            mu = jnp.sum(jnp.sum(y, axis=0), axis=-1, keepdims=True) / n_ch
            yc = y - mu[None]
            var = jnp.sum(jnp.sum(yc * yc, axis=0), axis=-1, keepdims=True) / n_ch
            inv = lax.rsqrt(var + LN_EPS)
            for cc in range(n_cc):
                cols = slice(cc * LANES, (cc + 1) * LANES)
                z = yc[cc] * inv * g_ref[:, cols] + b_ref[:, cols]
                o_ref[:, cols] = (z * jax.nn.sigmoid(z)).astype(o_ref.dtype)

    def halo_imap(b, t, c):
        return (jnp.maximum((b * seq + t * tm) // CONV_HALO - 1, 0), c)

    in_specs = [
        pl.BlockSpec((tm, LANES), lambda b, t, c: (b * nt + t, c)),
        pl.BlockSpec((CONV_HALO, LANES), halo_imap),
        pl.BlockSpec((CONV_WIDTH, LANES), lambda b, t, c: (0, c)),
        pl.BlockSpec((1, LANES), lambda b, t, c: (0, c)),
        pl.BlockSpec((1, n_ch), lambda b, t, c: (0, 0)),
        pl.BlockSpec((1, n_ch), lambda b, t, c: (0, 0)),
    ]
    del halo_blocks
    return pl.pallas_call(
        body, grid=(n_batch, nt, n_cc), in_specs=in_specs,
        out_specs=pl.BlockSpec((tm, n_ch), lambda b, t, c: (b * nt + t, 0)),
        out_shape=jax.ShapeDtypeStruct(u.shape, BF16),
        scratch_shapes=[pltpu.VMEM((tm + CONV_HALO, LANES), F32), pltpu.VMEM((n_cc, tm, LANES), F32)],
        compiler_params=_params(("parallel", "parallel", "arbitrary")), name=name)(
        u, u, w_dw, b_dw, g_n, b_n)


def _dwconv_decode(u, buf, w_dw, b_dw, g_n, b_n, *, name):
    n_b, n_ch = u.shape
    bt = 8

    def body(u_ref, buf_ref, w_ref, bd_ref, g_ref, b_ref, o_ref):
        w = w_ref[...]
        y = jnp.sum(buf_ref[...] * w[None, :CONV_WIDTH - 1, :], axis=1)
        y = y + u_ref[...] * w[CONV_WIDTH - 1:CONV_WIDTH, :] + bd_ref[...]
        z = _ln_rows(y, g_ref[...], b_ref[...])
        o_ref[...] = (z * jax.nn.sigmoid(z)).astype(o_ref.dtype)

    vec = pl.BlockSpec((1, n_ch), lambda i: (0, 0))
    return pl.pallas_call(
        body, grid=(n_b // bt,),
        in_specs=[pl.BlockSpec((bt, n_ch), lambda i: (i, 0)),
                  pl.BlockSpec((bt, CONV_WIDTH - 1, n_ch), lambda i: (i, 0, 0)),
                  pl.BlockSpec((CONV_WIDTH, n_ch), lambda i: (0, 0)), vec, vec, vec],
        out_specs=pl.BlockSpec((bt, n_ch), lambda i: (i, 0)),
        out_shape=jax.ShapeDtypeStruct((n_b, n_ch), BF16),
        compiler_params=_params(("parallel",)), name=name)(u, buf, w_dw, b_dw, g_n, b_n)


def _sgu_gate_prompt(uv, w_s, b_s_t, g_v, b_v, *, n_batch, seq, name):
    n_chunks = seq // CHUNK
    gd = SGU_GROUP_DIM

    def body(u_ref, v_ref, ws_ref, bs_ref, g_ref, b_ref, o_ref, vo_ref):
        vn = _ln_rows(v_ref[...], g_ref[...], b_ref[...])
        vo_ref[...] = vn
        vb = vn.astype(BF16)
        row = lax.broadcasted_iota(jnp.int32, (CHUNK, CHUNK), 0)
        col = lax.broadcasted_iota(jnp.int32, (CHUNK, CHUNK), 1)
        keep = col <= row
        for grp in range(N_SGU_GROUPS):
            cols = slice(grp * gd, (grp + 1) * gd)
            wm = jnp.where(keep, ws_ref[grp], 0.0).astype(BF16)
            gate = jnp.dot(wm, vb[:, cols], preferred_element_type=F32) + bs_ref[:, grp:grp + 1]
            o_ref[:, cols] = (u_ref[:, cols] * gate).astype(o_ref.dtype)

    nb = D_SGU // D_SGU
    in_specs = [
        pl.BlockSpec((CHUNK, D_SGU), lambda b, n: (b * n_chunks + n, 0)),
        pl.BlockSpec((CHUNK, D_SGU), lambda b, n: (b * n_chunks + n, nb)),
        pl.BlockSpec((N_SGU_GROUPS, CHUNK, CHUNK), lambda b, n: (0, 0, 0)),
        pl.BlockSpec((CHUNK, N_SGU_GROUPS), lambda b, n: (0, 0)),
        pl.BlockSpec((1, D_SGU), lambda b, n: (0, 0)),
        pl.BlockSpec((1, D_SGU), lambda b, n: (0, 0)),
    ]
    out_specs = [pl.BlockSpec((CHUNK, D_SGU), lambda b, n: (b * n_chunks + n, 0)),
                 pl.BlockSpec((None, CHUNK, D_SGU), lambda b, n: (b, 0, 0))]
    out_shape = [jax.ShapeDtypeStruct((n_batch * seq, D_SGU), BF16),
                 jax.ShapeDtypeStruct((n_batch, CHUNK, D_SGU), F32)]
    return pl.pallas_call(body, grid=(n_batch, n_chunks), in_specs=in_specs, out_specs=out_specs,
                          out_shape=out_shape, compiler_params=_params(("parallel", "arbitrary")),
                          name=name)(uv, uv, w_s, b_s_t, g_v, b_v)


def _sgu_gate_decode(uv, w_vec, b_vec, g_v, b_v, *, name):
    n_b = uv.shape[0]

    def body(u_ref, v_ref, w_ref, bs_ref, g_ref, b_ref, o_ref, vo_ref):
        vn = _ln_rows(v_ref[...], g_ref[...], b_ref[...])
        vo_ref[...] = vn
        o_ref[...] = (u_ref[...] * (vn * w_ref[...] + bs_ref[...])).astype(o_ref.dtype)

    vec = pl.BlockSpec((1, D_SGU), lambda i: (0, 0))
    return pl.pallas_call(
        body, grid=(1,),
        in_specs=[pl.BlockSpec((n_b, D_SGU), lambda i: (0, 0)), pl.BlockSpec((n_b, D_SGU), lambda i: (0, 1)),
                  vec, vec, vec, vec],
        out_specs=[pl.BlockSpec((n_b, D_SGU), lambda i: (0, 0)), pl.BlockSpec((n_b, D_SGU), lambda i: (0, 0))],
        out_shape=[jax.ShapeDtypeStruct((n_b, D_SGU), BF16), jax.ShapeDtypeStruct((n_b, D_SGU), F32)],
        compiler_params=_params(("arbitrary",)), name=name)(uv, uv, w_vec, b_vec, g_v, b_v)


def _router(x, mod, w_r, b_r, l, *, tm, name):
    m_rows = x.shape[0]
    marr, rows_per_b = mod
    r = marr.shape[1]
    n_e = N_EXPERTS

    def body(x_ref, sh_ref, sc_ref, w_ref, b_ref, h_ref, gates_ref, idx_ref):
        h = x_ref[...] * (1.0 + sc_ref[...]) + sh_ref[...]
        h_ref[...] = h
        logits = jnp.dot(h, w_ref[...], preferred_element_type=F32,
                         precision=lax.Precision.HIGHEST) + b_ref[...]
        lane = lax.broadcasted_iota(jnp.int32, logits.shape, 1)
        m1 = jnp.max(logits, axis=-1, keepdims=True)
        i1 = jnp.min(jnp.where(logits == m1, lane, n_e), axis=-1, keepdims=True)
        rest = jnp.where(lane == i1, -jnp.inf, logits)
        m2 = jnp.max(rest, axis=-1, keepdims=True)
        i2 = jnp.min(jnp.where(rest == m2, lane, n_e), axis=-1, keepdims=True)
        e2 = jnp.exp(m2 - m1)
        den = 1.0 + e2
        gates_ref[...] = jnp.where(lane == i1, 1.0 / den, 0.0) + jnp.where(lane == i2, e2 / den, 0.0)
        idx_ref[...] = jnp.where(lane == 0, i1, jnp.where(lane == 1, i2, 0))

    in_specs = [
        pl.BlockSpec((tm, D_MODEL), lambda i: (i, 0)),
        pl.BlockSpec((None, r, D_MODEL), lambda i: ((i * tm) // rows_per_b, 0, 3)),
        pl.BlockSpec((None, r, D_MODEL), lambda i: ((i * tm) // rows_per_b, 0, 4)),
        pl.BlockSpec((None, D_MODEL, n_e), lambda i: (l, 0, 0)),
        pl.BlockSpec((None, 1, n_e), lambda i: (l, 0, 0)),
    ]
    out_specs = [pl.BlockSpec((tm, D_MODEL), lambda i: (i, 0)),
                 pl.BlockSpec((tm, n_e), lambda i: (i, 0)),
                 pl.BlockSpec((tm, n_e), lambda i: (i, 0))]
    out_shape = [jax.ShapeDtypeStruct((m_rows, D_MODEL), F32),
                 jax.ShapeDtypeStruct((m_rows, n_e), F32),
                 jax.ShapeDtypeStruct((m_rows, n_e), jnp.int32)]
    return pl.pallas_call(body, grid=(m_rows // tm,), in_specs=in_specs, out_specs=out_specs,
                          out_shape=out_shape, compiler_params=_params(("parallel",)), name=name)(
        x, marr, marr, w_r, b_r)


def _gather_rows(h, row_src, n_used, *, tm, name):
    n_rows = row_src.shape[0]
    d = h.shape[1]

    def body(rs_ref, nu_ref, h_ref, o_ref, buf_ref, sems):
        i = pl.program_id(0)
        slot = i % 2

        def row_copy(tile_idx, slot_idx, r):
            src = rs_ref[tile_idx * tm + r]
            return pltpu.make_async_copy(h_ref.at[pl.ds(src, 1)], buf_ref.at[slot_idx, pl.ds(r, 1)],
                                         sems.at[slot_idx])

        def issue(tile_idx, slot_idx):
            def f(r, c):
                row_copy(tile_idx, slot_idx, r).start()
                return c
            lax.fori_loop(0, tm, f, 0, unroll=8)

        @pl.when(i == 0)
        def _():
            issue(0, 0)

        @pl.when(i + 1 < nu_ref[0])
        def _():
            issue(i + 1, 1 - slot)

        @pl.when(i < nu_ref[0])
        def _():
            def wait(r, c):
                row_copy(i, slot, r).wait()
                return c

            lax.fori_loop(0, tm, wait, 0, unroll=8)
            o_ref[...] = buf_ref[slot].astype(o_ref.dtype)

        @pl.when(i >= nu_ref[0])
        def _():
            o_ref[...] = jnp.zeros(o_ref.shape, o_ref.dtype)

    grid_spec = pltpu.PrefetchScalarGridSpec(
        num_scalar_prefetch=2, grid=(n_rows // tm,), in_specs=[pl.BlockSpec(memory_space=pl.ANY)],
        out_specs=pl.BlockSpec((tm, d), lambda i, rs, nu: (i, 0)),
        scratch_shapes=[pltpu.VMEM((2, tm, d), h.dtype), pltpu.SemaphoreType.DMA((2,))])
    return pl.pallas_call(body, grid_spec=grid_spec, out_shape=jax.ShapeDtypeStruct((n_rows, d), BF16),
                          compiler_params=_params(("arbitrary",)), name=name)(row_src, n_used, h)


def _moe_combine(expert_out, pos, wts, x, mod, ln_g, ln_b, ln_idx, *, tm, name):
    m_rows = x.shape[0]
    marr, rows_per_b = mod
    r = marr.shape[1]

    def body(pos_ref, eo_ref, w_ref, x_ref, gt_ref, g_ref, b_ref, o_ref, buf_ref, sem):
        base = pl.program_id(0) * tm

        def issue(t, c):
            for k in range(TOP_K):
                src = pos_ref[(base + t) * TOP_K + k]
                pltpu.make_async_copy(eo_ref.at[pl.ds(src, 1)], buf_ref.at[k, pl.ds(t, 1)], sem).start()
            return c

        lax.fori_loop(0, tm, issue, 0, unroll=4)

        def wait(t, c):
            for k in range(TOP_K):
                pltpu.make_async_copy(eo_ref.at[pl.ds(0, 1)], buf_ref.at[k, pl.ds(t, 1)], sem).wait()
            return c

        lax.fori_loop(0, tm, wait, 0, unroll=4)
        w = w_ref[...]
        f = w[:, 0:1] * buf_ref[0] + w[:, 1:2] * buf_ref[1]
        o_ref[...] = _ln_rows(ALPHA * x_ref[...] + (1.0 + gt_ref[...]) * f, g_ref[...], b_ref[...])

    in_specs = [
        pl.BlockSpec(memory_space=pl.ANY),
        pl.BlockSpec((tm, TOP_K), lambda i, p: (i, 0)),
        pl.BlockSpec((tm, D_MODEL), lambda i, p: (i, 0)),
        pl.BlockSpec((None, r, D_MODEL), lambda i, p: ((i * tm) // rows_per_b, 0, 5)),
        pl.BlockSpec((None, 1, D_MODEL), lambda i, p: (ln_idx, 0, 0)),
        pl.BlockSpec((None, 1, D_MODEL), lambda i, p: (ln_idx, 0, 0)),
    ]
    grid_spec = pltpu.PrefetchScalarGridSpec(
        num_scalar_prefetch=1, grid=(m_rows // tm,), in_specs=in_specs,
        out_specs=pl.BlockSpec((tm, D_MODEL), lambda i, p: (i, 0)),
        scratch_shapes=[pltpu.VMEM((TOP_K, tm, D_MODEL), F32), pltpu.SemaphoreType.DMA(())])
    return pl.pallas_call(body, grid_spec=grid_spec, out_shape=jax.ShapeDtypeStruct(x.shape, F32),
                          compiler_params=_params(("parallel",)), name=name)(
        pos, expert_out, wts, x, marr, ln_g, ln_b)


def _route_tables(idx, n_tokens, tm):
    n_pairs = n_tokens * TOP_K
    n_tiles = -(-(n_pairs + N_EXPERTS * (tm - 1)) // tm)
    n_tiles = -(-n_tiles // ROW_TILE_GROUP) * ROW_TILE_GROUP
    e_flat = idx[:, :TOP_K].reshape(-1)
    onehot = (e_flat[:, None] == jnp.arange(N_EXPERTS, dtype=jnp.int32)[None, :]).astype(jnp.int32)
    rank = jnp.take_along_axis(jnp.cumsum(onehot, axis=0), e_flat[:, None], axis=1)[:, 0] - 1
    counts = jnp.sum(onehot, axis=0)
    tiles_per = (counts + tm - 1) // tm
    tile_end = jnp.cumsum(tiles_per)
    tile_start = tile_end - tiles_per
    pos = (tile_start[e_flat] * tm + rank).astype(jnp.int32)
    row_src = jnp.zeros((n_tiles * tm,), jnp.int32).at[pos].set(
        jnp.arange(n_pairs, dtype=jnp.int32) // TOP_K)
    n_used = tile_end[-1].astype(jnp.int32)
    tile_ids = jnp.arange(n_tiles, dtype=jnp.int32)
    tile_e = jnp.sum((tile_ids[:, None] >= tile_end[None, :]).astype(jnp.int32), axis=1)
    last_e = jnp.sum((n_used - 1 >= tile_end).astype(jnp.int32))
    tile_e = jnp.where(tile_ids < n_used, tile_e, last_e).astype(jnp.int32)
    return row_src, tile_e, n_used.reshape(1), pos


def _rope_tables(pos, width_blocks):
    half = QK_ROPE_DIM // 2
    inv = ROPE_THETA ** (-jnp.arange(half, dtype=F32) / half)
    ang = pos.astype(F32)[:, None] * inv[None, :]
    zeros = jnp.zeros((pos.shape[0], LANES - QK_ROPE_DIM), F32)
    ta = jnp.concatenate([jnp.cos(ang), jnp.cos(ang), zeros], axis=1)
    tb = jnp.concatenate([jnp.sin(ang), jnp.sin(ang), zeros], axis=1)
    return jnp.tile(ta, (1, width_blocks)), jnp.tile(tb, (1, width_blocks))


def _rot_cols(w):
    half = QK_ROPE_DIM // 2
    return jnp.concatenate([-w[..., half:], w[..., :half]], axis=-1)


def _pad_lanes(w):
    return jnp.concatenate([w, jnp.zeros(w.shape[:-1] + (LANES - w.shape[-1],), w.dtype)], axis=-1)


def kernel(x_prompt, x_sample, cache_latent, cache_krope, state_conv, page_table, c_prompt, c_sample,
           w_ada, b_ada, ln_g, ln_b,
           mla_w_dqkv, mla_g_q, mla_g_kv, mla_w_uq, mla_w_qr, mla_w_uk, mla_w_uv, mla_w_o,
           conv_w_pw1, conv_b_pw1, conv_w_dw, conv_b_dw, conv_g_n, conv_b_n, conv_w_pw2, conv_b_pw2,
           sgu_w_in, sgu_b_in, sgu_g_v, sgu_b_v, sgu_w_s, sgu_b_s, sgu_w_out, sgu_b_out,
           ffn_w_gu, ffn_w_down, moe_w_r, moe_b_r, moe_w_gu, moe_w_down):
    n_batch, seq, d = x_prompt.shape
    n_dec = x_sample.shape[0]
    assert x_sample.shape[1] == 1 and d == D_MODEL
    past = page_table.shape[1] * cache_latent.shape[2]
    t_p = n_batch * seq
    tm_p = min(512, seq)
    tm_ff = min(1024, seq)
    tn_up = 1024
    tm_moe = min(512, seq)

    n_c = n_batch + n_dec
    n_c_pad = -(-n_c // 16) * 16
    c_all = jnp.concatenate([c_prompt, c_sample, jnp.zeros((n_c_pad - n_c, d), F32)], axis=0)
    b_ada3 = b_ada.reshape(DEPTH, 1, N_MOD * d)
    mods_p, mods_s = [], []
    for l in range(DEPTH):
        tn = 1024
        (mod_l,) = _proj(
            c_all, [(w_ada, (l,), 0)], lambda ys, es: ys,
            [((n_c_pad, N_MOD * d), F32, (n_c_pad, tn), lambda i, j: (0, j))],
            tm=n_c_pad, tn=tn, nj=N_MOD * d // tn, bs=[(b_ada3, l, 0)], silu_in=True, name=f"ada{l}")
        mods_p.append(mod_l[:n_batch].reshape(n_batch, 1, N_MOD * d))
        mods_s.append(mod_l[n_batch:n_c].reshape(1, n_dec, N_MOD * d))

    ln_g3 = ln_g.reshape(DEPTH * 2, 1, d)
    ln_b3 = ln_b.reshape(DEPTH * 2, 1, d)

    n_a = mla_w_dqkv.shape[0]
    hq = N_HEADS * QK_NOPE_DIM
    w_uq2 = mla_w_uq.reshape(n_a, Q_LORA_RANK, hq)
    w_uk2 = mla_w_uk.reshape(n_a, KV_LORA_RANK, hq)
    w_uv2 = mla_w_uv.reshape(n_a, KV_LORA_RANK, N_HEADS * V_HEAD_DIM)
    w_qr_a = _pad_lanes(mla_w_qr).reshape(n_a, Q_LORA_RANK, N_HEADS * LANES)
    w_qr_b = _pad_lanes(_rot_cols(mla_w_qr)).reshape(n_a, Q_LORA_RANK, N_HEADS * LANES)
    w_kr = mla_w_dqkv[:, :, Q_LORA_RANK + KV_LORA_RANK:]
    w_kr_a = _pad_lanes(w_kr)
    w_kr_b = _pad_lanes(_rot_cols(w_kr))
    g_q3 = mla_g_q.reshape(n_a, 1, Q_LORA_RANK)
    g_kv3 = mla_g_kv.reshape(n_a, 1, KV_LORA_RANK)
    n_ff = ffn_w_gu.shape[0]
    n_moe = moe_w_r.shape[0]
    b_r3 = moe_b_r.reshape(n_moe, 1, N_EXPERTS)

    def mla_front(x, mod, rows_per_b, tm_down, tm, tabs, tabs_wide, j, tag):
        ta, tb = tabs
        c_q, c_kv, kr = _mla_down(x, (mod, rows_per_b), mla_w_dqkv, j, w_kr_a[j], w_kr_b[j],
                                  g_q3, g_kv3, ta, tb, tm=tm_down, name=f"mla_down_{tag}")
        m_rows = x.shape[0]
        tn = tn_up
        n_tab = ta.shape[0] // tm
        (qn,) = _proj(c_q, [(w_uq2, (j,), 0)], lambda ys, es: ys,
                      [((m_rows, hq), BF16 if tag[0] == 'p' else F32, (tm, tn), lambda i, jj: (i, jj))],
                      tm=tm, tn=tn, nj=hq // tn, name=f"mla_qn_{tag}")
        taw, tbw = tabs_wide
        (qr,) = _proj(c_q, [(w_qr_a, (j,), 0), (w_qr_b, (j,), 0)],
                      lambda ys, es: [ys[0] * es[0] + ys[1] * es[1]],
                      [((m_rows, hq), BF16 if tag[0] == 'p' else F32, (tm, tn), lambda i, jj: (i, jj))],
                      tm=tm, tn=tn, nj=hq // tn,
                      extras=[(taw, (tm, tn), lambda i, jj: (i % n_tab, 0)),
                              (tbw, (tm, tn), lambda i, jj: (i % n_tab, 0))],
                      name=f"mla_qr_{tag}")
        return c_q, c_kv, kr, qn, qr

    def grp(n_row_tiles, want):
        return max(c for c in range(1, want + 1) if n_row_tiles % c == 0)

    def res_args(x, mod, comp, rows_per_b, l, s):
        return (x, mod, comp, rows_per_b, ln_g3, ln_b3, 2 * l + s)

    def dense_ffn(x, mod, rows_per_b, tm_a, tm_b, l, tag):
        jf = l // 2
        tn = 256
        nb = DENSE_FF // tn
        (act,) = _proj(x, [(ffn_w_gu, (jf,), 0), (ffn_w_gu, (jf,), nb)],
                       lambda ys, es: [ys[0] * jax.nn.sigmoid(ys[0]) * ys[1]],
                       [((x.shape[0], DENSE_FF), BF16, (tm_a, tn), lambda i, jj: (i, jj))],
                       tm=tm_a, tn=tn, nj=nb, mod=(mod, 3, 4, rows_per_b), name=f"ffn_gu_{tag}")
        return _proj_k(act, ffn_w_down, (jf,), tm=tm_b, tk=512, group=grp(x.shape[0] // tm_b, 4),
                       res=res_args(x, mod, 5, rows_per_b, l, 1), name=f"ffn_down_{tag}")

    pos_p = jnp.arange(seq, dtype=jnp.int32)
    pos_s = jnp.full((n_dec,), past, dtype=jnp.int32)
    tabs_p, tabs_s = _rope_tables(pos_p, 1), _rope_tables(pos_s, 1)
    tabs_pw, tabs_sw = _rope_tables(pos_p, tn_up // LANES), _rope_tables(pos_s, tn_up // LANES)

    xp = x_prompt.reshape(t_p, d)
    xs = x_sample.reshape(n_dec, d)
    cache_kr_t = jnp.swapaxes(cache_krope, 2, 3)
    lat_p, kr_p, lat_s, kr_s = [], [], [], []
    conv_p = conv_s = v_p = v_s = None

    for l in range(DEPTH):
        kind, j = l % 3, l // 3
        mp, ms = mods_p[l], mods_s[l]
        if kind == 0:
            c_q, c_kv, kr, qn, qr = mla_front(xp, mp, seq, tm_p, tm_ff, tabs_p, tabs_pw, j, f"p{l}")
            tn = tn_up
            (kn,) = _proj(c_kv, [(w_uk2, (j,), 0)], lambda ys, es: ys,
                          [((t_p, hq), BF16, (tm_ff, tn), lambda i, jj: (i, jj))],
                          tm=tm_ff, tn=tn, nj=hq // tn, name=f"mla_kn_p{l}")
            (vv,) = _proj(c_kv, [(w_uv2, (j,), 0)], lambda ys, es: ys,
                          [((t_p, hq), BF16, (tm_ff, tn), lambda i, jj: (i, jj))],
                          tm=tm_ff, tn=tn, nj=hq // tn, name=f"mla_v_p{l}")
            o = _attn_prompt(qn, qr, kn, kr, vv, n_batch=n_batch, seq=seq, tq=min(512, seq),
                             name=f"attn_p{l}")
            xp = _proj_k(o, mla_w_o, (j,), tm=tm_p, tk=1024, group=grp(t_p // tm_p, 2),
                         res=res_args(xp, mp, 2, seq, l, 0),
                         name=f"mla_o_p{l}")
            lat_p.append(c_kv.reshape(n_batch, seq, KV_LORA_RANK))
            kr_p.append(kr[:, :QK_ROPE_DIM].reshape(n_batch, seq, QK_ROPE_DIM))
            c_q, c_kv, kr, qn, qr = mla_front(xs, ms, n_dec, n_dec, n_dec, tabs_s, tabs_sw, j, f"s{l}")
            q_lat = _head_mix(qn, w_uk2, j, contract_w_last=True, name=f"mla_qlat_s{l}")
            o_lat = _attn_decode(q_lat.reshape(n_dec, N_HEADS, KV_LORA_RANK),
                                 qr.reshape(n_dec, N_HEADS, LANES),
                                 c_kv.reshape(n_dec, 1, KV_LORA_RANK), kr.reshape(n_dec, 1, LANES),
                                 cache_latent, cache_kr_t, page_table, j, name=f"attn_s{l}")
            o = _head_mix(o_lat.reshape(n_dec, N_HEADS * KV_LORA_RANK), w_uv2, j,
                          contract_w_last=False, name=f"mla_ov_s{l}")
            xs = _proj_k(o, mla_w_o, (j,), tm=n_dec, tk=1024, res=res_args(xs, ms, 2, n_dec, l, 0),
                         name=f"mla_o_s{l}")
            lat_s.append(c_kv.reshape(n_dec, 1, KV_LORA_RANK))
            kr_s.append(kr[:, :QK_ROPE_DIM].reshape(n_dec, 1, QK_ROPE_DIM))
        elif kind == 1:
            d_c = conv_w_pw1.shape[2] // 2
            b1 = conv_b_pw1.reshape(-1, 1, 2 * d_c)
            b2 = conv_b_pw2.reshape(-1, 1, d)
            tn = 256
            glu = lambda ys, es: [ys[0] * jax.nn.sigmoid(ys[1])]

            def pw1(x, mod, rows_per_b, tm, tag):
                return _proj(x, [(conv_w_pw1, (j,), 0), (conv_w_pw1, (j,), d_c // tn)], glu,
                             [((x.shape[0], d_c), F32, (tm, tn), lambda i, jj: (i, jj))],
                             tm=tm, tn=tn, nj=d_c // tn, bs=[(b1, j, 0), (b1, j, d_c // tn)],
                             mod=(mod, 0, 1, rows_per_b), name=f"conv_pw1_{tag}")[0]

            u = pw1(xp, mp, seq, tm_ff, f"p{l}")
            y = _dwconv_prompt(u, conv_w_dw[j], conv_b_dw[j].reshape(1, d_c), conv_g_n[j].reshape(1, d_c),
                               conv_b_n[j].reshape(1, d_c), n_batch=n_batch, seq=seq, tm=tm_p,
                               name=f"dwconv_p{l}")
            xp = _proj_k(y, conv_w_pw2, (j,), tm=tm_p, tk=1024, group=grp(t_p // tm_p, 2), bias=(b2, j),
                         res=res_args(xp, mp, 2, seq, l, 0), name=f"conv_pw2_p{l}")
            conv_p = u.reshape(n_batch, seq, d_c)[:, seq - (CONV_WIDTH - 1):][None]

            u = pw1(xs, ms, n_dec, n_dec, f"s{l}")
            y = _dwconv_decode(u, state_conv[j], conv_w_dw[j], conv_b_dw[j].reshape(1, d_c),
                               conv_g_n[j].reshape(1, d_c), conv_b_n[j].reshape(1, d_c), name=f"dwconv_s{l}")
            xs = _proj_k(y, conv_w_pw2, (j,), tm=n_dec, tk=1024, bias=(b2, j),
                         res=res_args(xs, ms, 2, n_dec, l, 0), name=f"conv_pw2_s{l}")
            conv_s = jnp.concatenate([state_conv[j][:, 1:], u[:, None, :]], axis=1)[None]
        else:
            b_in = sgu_b_in.reshape(-1, 1, 2 * D_SGU)
            b_out = sgu_b_out.reshape(-1, 1, d)
            tn = 512
            gelu = lambda ys, es: [_gelu_tanh(ys[0])]

            def sgu_in(x, mod, rows_per_b, tm, tag):
                return _proj(x, [(sgu_w_in, (j,), 0)], gelu,
                             [((x.shape[0], 2 * D_SGU), F32, (tm, tn), lambda i, jj: (i, jj))],
                             tm=tm, tn=tn, nj=2 * D_SGU // tn, bs=[(b_in, j, 0)],
                             mod=(mod, 0, 1, rows_per_b), name=f"sgu_in_{tag}")[0]

            g_v = sgu_g_v[j].reshape(1, D_SGU)
            b_v = sgu_b_v[j].reshape(1, D_SGU)
            uv = sgu_in(xp, mp, seq, tm_ff, f"p{l}")
            ug, v_last = _sgu_gate_prompt(uv, sgu_w_s[j], sgu_b_s[j].T, g_v, b_v, n_batch=n_batch, seq=seq,
                                          name=f"sgu_gate_p{l}")
            xp = _proj_k(ug, sgu_w_out, (j,), tm=tm_p, tk=1024, group=grp(t_p // tm_p, 2), bias=(b_out, j),
                         res=res_args(xp, mp, 2, seq, l, 0), name=f"sgu_out_p{l}")
            v_p = v_last[None]

            uv = sgu_in(xs, ms, n_dec, n_dec, f"s{l}")
            w_vec = jnp.repeat(sgu_w_s[j][:, 0, 0], SGU_GROUP_DIM).reshape(1, D_SGU)
            b_vec = jnp.repeat(sgu_b_s[j][:, 0], SGU_GROUP_DIM).reshape(1, D_SGU)
            ug, vn = _sgu_gate_decode(uv, w_vec, b_vec, g_v, b_v, name=f"sgu_gate_s{l}")
            xs = _proj_k(ug, sgu_w_out, (j,), tm=n_dec, tk=1024, bias=(b_out, j),
                         res=res_args(xs, ms, 2, n_dec, l, 0), name=f"sgu_out_s{l}")
            v_s = vn.reshape(n_dec, 1, D_SGU)[None]

        if l % 2 == 0:
            xp = dense_ffn(xp, mp, seq, tm_ff, tm_p, l, f"p{l}")
            xs = dense_ffn(xs, ms, n_dec, n_dec, n_dec, l, f"s{l}")
        else:
            jm = l // 2
            h_p, gates_p, idx_p = _router(xp, (mp, seq), moe_w_r, b_r3, jm, tm=tm_p, name=f"router_p{l}")
            h_s, gates_s, idx_s = _router(xs, (ms, n_dec), moe_w_r, b_r3, jm, tm=n_dec, name=f"router_s{l}")
            h_all = jnp.concatenate([h_p, h_s], axis=0)
            idx_all = jnp.concatenate([idx_p, idx_s], axis=0)
            gates_all = jnp.concatenate([gates_p, gates_s], axis=0)
            n_tok = t_p + n_dec
            row_src, tile_e, n_used, pos = _route_tables(idx_all, n_tok, tm_moe)
            wts = jnp.take_along_axis(gates_all, idx_all[:, :TOP_K], axis=1)
            n_rows = row_src.shape[0]
            tn = 512
            nb = EXPERT_FF // tn
            h_sorted = _gather_rows(h_all, row_src, n_used, tm=tm_moe, name=f"moe_gather{l}")
            (act,) = _proj(h_sorted, [(moe_w_gu, (jm, 'e'), 0), (moe_w_gu, (jm, 'e'), nb)],
                           lambda ys, es: [ys[0] * jax.nn.sigmoid(ys[0]) * ys[1]],
                           [((n_rows, EXPERT_FF), BF16, (tm_moe, tn), lambda i, jj: (i, jj))],
                           tm=tm_moe, tn=tn, nj=nb, expert=(tile_e, n_used), cols_outer=True,
                           name=f"moe_gu{l}")
            eo = _proj_k(act, moe_w_down, (jm, 'e'), tm=tm_moe, tk=1024, group=ROW_TILE_GROUP,
                         expert=(tile_e, n_used),
                         name=f"moe_down{l}")
            xp = _moe_combine(eo, pos[:t_p * TOP_K], wts[:t_p], xp, (mp, seq), ln_g3, ln_b3, 2 * l + 1,
                              tm=tm_p, name=f"moe_comb_p{l}")
            xs = _moe_combine(eo, pos[t_p * TOP_K:], wts[t_p:], xs, (ms, n_dec), ln_g3, ln_b3, 2 * l + 1,
                              tm=n_dec, name=f"moe_comb_s{l}")

    return (xp.reshape(n_batch, seq, d), xs.reshape(n_dec, 1, d),
            jnp.stack(lat_p), jnp.stack(kr_p), conv_p, v_p,
            jnp.stack(lat_s), jnp.stack(kr_s), conv_s, v_s)
```

```python
import functools

import jax
import jax.numpy as jnp
from jax import lax
from jax.experimental import pallas as pl
from jax.experimental.pallas import tpu as pltpu

F32 = jnp.float32
BF16 = jnp.bfloat16

D_MODEL = 2048
DEPTH = 4
PAGE_SIZE = 128
N_HEADS = 16
QK_NOPE_DIM = 128
QK_ROPE_DIM = 64
V_HEAD_DIM = 128
Q_LORA_RANK = 512
KV_LORA_RANK = 512
ROPE_THETA = 10000.0
ATTN_SCALE = (QK_NOPE_DIM + QK_ROPE_DIM) ** -0.5
CONV_WIDTH = 31
CHUNK = 128
D_SGU = 2 * D_MODEL
N_SGU_GROUPS = 16
SGU_GROUP_DIM = D_SGU // N_SGU_GROUPS
DENSE_FF = 5632
N_EXPERTS = 8
TOP_K = 2
EXPERT_FF = 7168
N_MOD = 6
ALPHA = (2 * DEPTH) ** 0.25
LN_EPS = 1e-5
RMS_EPS = 1e-6

LANES = 128
V7X_VMEM_LIMIT = 56 * 1024 * 1024
CONV_HALO = 32
PAGES_PER_STEP = 16
ROW_TILE_GROUP = 4
SEQS_PER_STEP = 2


def _params(sem):
    return pltpu.CompilerParams(dimension_semantics=sem, vmem_limit_bytes=V7X_VMEM_LIMIT)


def _ln_rows(v, g, b):
    mu = jnp.mean(v, axis=-1, keepdims=True)
    vc = v - mu
    var = jnp.mean(vc * vc, axis=-1, keepdims=True)
    return vc * lax.rsqrt(var + LN_EPS) * g + b


def _gelu_tanh(y):
    return 0.5 * y * (1.0 + jnp.tanh(0.7978845608028654 * (y + 0.044715 * (y * y * y))))


def _proj(x, ws, epi, outs, *, tm, tn, nj, bs=None, mod=None, extras=(), silu_in=False,
          expert=None, cols_outer=False, name):
    if bs is None:
        bs = [None] * len(ws)
    m_rows, k_dim = x.shape
    ni = m_rows // tm
    assert ni * tm == m_rows
    has_mod = mod is not None
    prefetch = list(expert) if expert is not None else []
    n_pf = len(prefetch)
    direct = (x.dtype == BF16 and not has_mod and not silu_in)
    assert direct or not cols_outer

    def order(f):
        if cols_outer:
            return lambda j, i, *pf: f(i, j, *pf)
        return f

    def x_imap(i, j, *pf):
        if expert is not None:
            i = jnp.minimum(i, pf[1][0] - 1)
        return (i, 0)

    def w_imap(lead, off):
        def f(i, j, *pf):
            jj = j
            if expert is not None and not cols_outer:
                jj = jnp.where(i < pf[1][0], j, nj - 1)
            return tuple(pf[0][i] if s == 'e' else s for s in lead) + (0, off + jj)
        return f

    in_specs = [pl.BlockSpec((tm, k_dim), order(x_imap))]
    args = [x]
    if has_mod:
        marr, c_sh, c_sc, rows_per_b = mod
        r = marr.shape[1]
        for comp in (c_sh, c_sc):
            in_specs.append(pl.BlockSpec(
                (None, r, k_dim), order(lambda i, j, *pf, comp=comp: ((i * tm) // rows_per_b, 0, comp))))
            args.append(marr)
    for arr, lead, off in ws:
        in_specs.append(pl.BlockSpec((None,) * len(lead) + (k_dim, tn), order(w_imap(lead, off))))
        args.append(arr)
    for b in bs:
        if b is not None:
            arr, lead, off = b
            in_specs.append(pl.BlockSpec(
                (None, 1, tn), order(lambda i, j, *pf, lead=lead, off=off: (lead, 0, off + j))))
            args.append(arr)
    for arr, bshape, imap in extras:
        in_specs.append(pl.BlockSpec(bshape, order(lambda i, j, *pf, imap=imap: imap(i, j))))
        args.append(arr)
    out_specs = [pl.BlockSpec(bshape, order(lambda i, j, *pf, imap=imap: imap(i, j)))
                 for (_, _, bshape, imap) in outs]
    out_shape = [jax.ShapeDtypeStruct(shape, dtype) for (shape, dtype, _, _) in outs]
    scratch = [] if direct else [pltpu.VMEM((tm, k_dim), BF16)]
    if cols_outer:
        scratch.append(pltpu.VMEM((len(ws), k_dim, tn), BF16))

    def body(*refs):
        it = iter(refs)
        pf = [next(it) for _ in range(n_pf)]
        x_ref = next(it)
        sh_ref = sc_ref = None
        if has_mod:
            sh_ref, sc_ref = next(it), next(it)
        w_refs = [next(it) for _ in ws]
        b_refs = [next(it) if b is not None else None for b in bs]
        e_refs = [next(it) for _ in extras]
        o_refs = [next(it) for _ in outs]
        h_ref = None if direct else next(it)
        wb_ref = next(it) if cols_outer else None
        i, j = pl.program_id(0), pl.program_id(1)
        if cols_outer:
            i, j = j, i
            new_block = i == 0
            if expert is not None:
                new_block = jnp.logical_or(new_block, pf[0][i] != pf[0][jnp.maximum(i - 1, 0)])

            @pl.when(new_block)
            def _():
                for v, w_ref in enumerate(w_refs):
                    wb_ref[v] = w_ref[...].astype(BF16)

        if not direct:
            @pl.when(j == 0)
            def _():
                h = x_ref[...].astype(F32)
                if has_mod:
                    h = h * (1.0 + sc_ref[...]) + sh_ref[...]
                if silu_in:
                    h = h * jax.nn.sigmoid(h)
                h_ref[...] = h.astype(BF16)

        def compute():
            h = x_ref[...] if direct else h_ref[...]
            ys = []
            for v, (w_ref, b_ref) in enumerate(zip(w_refs, b_refs)):
                w_bf = wb_ref[v] if cols_outer else w_ref[...].astype(BF16)
                y = jnp.dot(h, w_bf, preferred_element_type=F32)
                if b_ref is not None:
                    y = y + b_ref[...]
                ys.append(y)
            res = epi(ys, [e[...] for e in e_refs])
            for o_ref, r_ in zip(o_refs, res):
                o_ref[...] = r_.astype(o_ref.dtype)

        if expert is not None:
            @pl.when(i < pf[1][0])
            def _():
                compute()

            @pl.when(i >= pf[1][0])
            def _():
                for o_ref in o_refs:
                    o_ref[...] = jnp.zeros(o_ref.shape, o_ref.dtype)
        else:
            compute()

    grid = (nj, ni) if cols_outer else (ni, nj)
    sem = ("arbitrary", "arbitrary") if cols_outer else ("parallel", "arbitrary")
    grid_spec = pltpu.PrefetchScalarGridSpec(
        num_scalar_prefetch=n_pf, grid=grid, in_specs=in_specs, out_specs=out_specs,
        scratch_shapes=scratch)
    return pl.pallas_call(body, grid_spec=grid_spec, out_shape=out_shape,
                          compiler_params=_params(sem), name=name)(*prefetch, *args)


def _proj_k(a, w, lead, *, tm, tk, group=1, res=None, bias=None, expert=None, name):
    m_rows, k_dim = a.shape
    n_dim = w.shape[-1]
    ni, nk = m_rows // tm, k_dim // tk
    assert ni * tm == m_rows and nk * tk == k_dim and ni % group == 0
    prefetch = list(expert) if expert is not None else []
    n_pf = len(prefetch)

    def tile(g, t):
        return g * group + t

    def k_block(i, k, pf):
        return jnp.where(i < pf[1][0], k, nk - 1) if expert is not None else k

    def a_imap(g, k, t, *pf):
        i = tile(g, t)
        return (i, k_block(i, k, pf))

    def w_imap(g, k, t, *pf):
        i = tile(g, t)
        return tuple(pf[0][i] if s == 'e' else s for s in lead) + (k_block(i, k, pf), 0)

    def row_imap(g, k, t, *pf):
        return (jnp.where(k == nk - 1, tile(g, t), tile(g, 0)), 0)

    in_specs = [pl.BlockSpec((tm, tk), a_imap),
                pl.BlockSpec((None,) * len(lead) + (tk, n_dim), w_imap)]
    args = [a, w]
    if bias is not None:
        barr, bl = bias
        in_specs.append(pl.BlockSpec((None, 1, n_dim), lambda g, k, t, *pf: (bl, 0, 0)))
        args.append(barr)
    if res is not None:
        x_res, marr, c_gate, rows_per_b, ln_g, ln_b, ln_idx = res
        r = marr.shape[1]
        in_specs += [
            pl.BlockSpec((tm, n_dim), row_imap),
            pl.BlockSpec((None, r, n_dim),
                         lambda g, k, t, *pf: ((tile(g, t) * tm) // rows_per_b, 0, c_gate)),
            pl.BlockSpec((None, 1, n_dim), lambda g, k, t, *pf: (ln_idx, 0, 0)),
            pl.BlockSpec((None, 1, n_dim), lambda g, k, t, *pf: (ln_idx, 0, 0)),
        ]
        args += [x_res, marr, ln_g, ln_b]

    def body(*refs):
        it = iter(refs)
        pf = [next(it) for _ in range(n_pf)]
        a_ref, w_ref = next(it), next(it)
        b_ref = next(it) if bias is not None else None
        if res is not None:
            x_ref, gt_ref, g_ref, be_ref = next(it), next(it), next(it), next(it)
        o_ref, acc_ref = next(it), next(it)
        wb_ref = next(it) if group > 1 else None
        k, t = pl.program_id(1), pl.program_id(2)
        i = tile(pl.program_id(0), t)

        if group > 1:
            new_block = t == 0
            if expert is not None:
                new_block = jnp.logical_or(new_block, pf[0][i] != pf[0][jnp.maximum(i - 1, 0)])

            @pl.when(new_block)
            def _():
                wb_ref[...] = w_ref[...].astype(BF16)

        def compute():
            @pl.when(k == 0)
            def _():
                acc_ref[t] = jnp.zeros((tm, n_dim), F32)

            w_bf = wb_ref[...] if group > 1 else w_ref[...].astype(BF16)
            acc_ref[t] += jnp.dot(a_ref[...].astype(BF16), w_bf, preferred_element_type=F32)

            @pl.when(k == nk - 1)
            def _():
                y = acc_ref[t]
                if b_ref is not None:
                    y = y + b_ref[...]
                if res is not None:
                    y = _ln_rows(ALPHA * x_ref[...] + (1.0 + gt_ref[...]) * y, g_ref[...], be_ref[...])
                o_ref[...] = y.astype(o_ref.dtype)

        if expert is not None:
            @pl.when(i < pf[1][0])
            def _():
                compute()

            @pl.when(jnp.logical_and(i >= pf[1][0], k == nk - 1))
            def _():
                o_ref[...] = jnp.zeros(o_ref.shape, o_ref.dtype)
        else:
            compute()

    grid_spec = pltpu.PrefetchScalarGridSpec(
        num_scalar_prefetch=n_pf, grid=(ni // group, nk, group), in_specs=in_specs,
        out_specs=pl.BlockSpec((tm, n_dim), row_imap),
        scratch_shapes=[pltpu.VMEM((group, tm, n_dim), F32)]
        + ([pltpu.VMEM((tk, n_dim), BF16)] if group > 1 else []))
    return pl.pallas_call(body, grid_spec=grid_spec,
                          out_shape=jax.ShapeDtypeStruct((m_rows, n_dim), F32),
                          compiler_params=_params(("parallel", "arbitrary", "arbitrary")),
                          name=name)(*prefetch, *args)


def _mla_down(x, mod, w_dqkv, l, w_ka, w_kb, g_q, g_kv, tab_a, tab_b, *, tm, name):
    m_rows = x.shape[0]
    ni = m_rows // tm
    marr, rows_per_b = mod
    r = marr.shape[1]
    n_tab = tab_a.shape[0] // tm

    def body(x_ref, sh_ref, sc_ref, wq_ref, wkv_ref, wka_ref, wkb_ref, gq_ref, gkv_ref, ta_ref, tb_ref,
             cq_ref, ckv_ref, kr_ref):
        h = (x_ref[...] * (1.0 + sc_ref[...]) + sh_ref[...]).astype(BF16)

        def rms(w_ref, g_ref):
            z = jnp.dot(h, w_ref[...].astype(BF16), preferred_element_type=F32)
            return z * lax.rsqrt(jnp.mean(z * z, axis=-1, keepdims=True) + RMS_EPS) * g_ref[...]

        cq_ref[...] = rms(wq_ref, gq_ref).astype(cq_ref.dtype)
        ckv_ref[...] = rms(wkv_ref, gkv_ref)
        ka = jnp.dot(h, wka_ref[...].astype(BF16), preferred_element_type=F32)
        kb = jnp.dot(h, wkb_ref[...].astype(BF16), preferred_element_type=F32)
        kr_ref[...] = ka * ta_ref[...] + kb * tb_ref[...]

    rank = Q_LORA_RANK
    in_specs = [
        pl.BlockSpec((tm, D_MODEL), lambda i: (i, 0)),
        pl.BlockSpec((None, r, D_MODEL), lambda i: ((i * tm) // rows_per_b, 0, 0)),
        pl.BlockSpec((None, r, D_MODEL), lambda i: ((i * tm) // rows_per_b, 0, 1)),
        pl.BlockSpec((None, D_MODEL, rank), lambda i: (l, 0, 0)),
        pl.BlockSpec((None, D_MODEL, rank), lambda i: (l, 0, 1)),
        pl.BlockSpec((D_MODEL, LANES), lambda i: (0, 0)),
        pl.BlockSpec((D_MODEL, LANES), lambda i: (0, 0)),
        pl.BlockSpec((None, 1, rank), lambda i: (l, 0, 0)),
        pl.BlockSpec((None, 1, rank), lambda i: (l, 0, 0)),
        pl.BlockSpec((tm, LANES), lambda i: (i % n_tab, 0)),
        pl.BlockSpec((tm, LANES), lambda i: (i % n_tab, 0)),
    ]
    out_specs = [pl.BlockSpec((tm, rank), lambda i: (i, 0)),
                 pl.BlockSpec((tm, rank), lambda i: (i, 0)),
                 pl.BlockSpec((tm, LANES), lambda i: (i, 0))]
    out_shape = [jax.ShapeDtypeStruct((m_rows, rank), BF16),
                 jax.ShapeDtypeStruct((m_rows, rank), F32),
                 jax.ShapeDtypeStruct((m_rows, LANES), F32)]
    return pl.pallas_call(body, grid=(ni,), in_specs=in_specs, out_specs=out_specs, out_shape=out_shape,
                          compiler_params=_params(("parallel",)), name=name)(
        x, marr, marr, w_dqkv, w_dqkv, w_ka, w_kb, g_q, g_kv, tab_a, tab_b)


def _attn_prompt(qn, qr, kn, kr, v, *, n_batch, seq, tq, name):
    hd = QK_NOPE_DIM
    n_q = seq // tq

    def body(qn_ref, qr_ref, kn_ref, kr_ref, v_ref, o_ref):
        k_all = jnp.concatenate([kn_ref[...], kr_ref[...].astype(BF16)], axis=1)
        for qi in range(n_q):
            rows = slice(qi * tq, (qi + 1) * tq)
            n_kv = (qi + 1) * tq
            q = jnp.concatenate([qn_ref[rows, :], qr_ref[rows, :]], axis=1)
            s = lax.dot_general(q, k_all[:n_kv], (((1,), (1,)), ((), ())),
                                preferred_element_type=F32) * ATTN_SCALE
            row = lax.broadcasted_iota(jnp.int32, (tq, n_kv), 0) + qi * tq
            col = lax.broadcasted_iota(jnp.int32, (tq, n_kv), 1)
            s = jnp.where(col <= row, s, -jnp.inf)
            m = jnp.max(s, axis=-1, keepdims=True)
            p = jnp.exp(s - m)
            den = jnp.sum(p, axis=-1, keepdims=True)
            o = jnp.dot(p.astype(BF16), v_ref[0:n_kv, :], preferred_element_type=F32) / den
            o_ref[rows, :] = o.astype(o_ref.dtype)

    hspec = pl.BlockSpec((seq, hd), lambda b, h: (b, h))
    in_specs = [hspec, hspec, hspec, pl.BlockSpec((seq, LANES), lambda b, h: (b, 0)), hspec]
    return pl.pallas_call(body, grid=(n_batch, N_HEADS), in_specs=in_specs, out_specs=hspec,
                          out_shape=jax.ShapeDtypeStruct(qn.shape, BF16),
                          compiler_params=_params(("parallel", "parallel")), name=name)(qn, qr, kn, kr, v)


def _head_mix(a, w, l, *, contract_w_last, name):
    n_b = a.shape[0]
    c_dim = w.shape[1]
    hd = QK_NOPE_DIM

    def body(a_ref, w_ref, o_ref):
        wv = w_ref[...].astype(BF16)
        av = a_ref[...].astype(BF16)
        if contract_w_last:
            o_ref[...] = lax.dot_general(av, wv, (((1,), (1,)), ((), ())), preferred_element_type=F32)
        else:
            o_ref[...] = jnp.dot(av, wv, preferred_element_type=F32)

    a_w, o_w = (hd, c_dim) if contract_w_last else (c_dim, hd)
    return pl.pallas_call(
        body, grid=(N_HEADS,),
        in_specs=[pl.BlockSpec((n_b, a_w), lambda h: (0, h)),
                  pl.BlockSpec((None, c_dim, hd), lambda h: (l, 0, h))],
        out_specs=pl.BlockSpec((n_b, o_w), lambda h: (0, h)),
        out_shape=jax.ShapeDtypeStruct((n_b, N_HEADS * o_w), F32),
        compiler_params=_params(("parallel",)), name=name)(a, w)


def _attn_decode(q_lat, q_rope, c_new, kr_new, cache_lat, cache_kr, page_table, l, *, name):
    n_b, n_pages = page_table.shape
    g, nq = PAGES_PER_STEP, SEQS_PER_STEP
    n_steps = n_pages // g
    assert n_steps * g == n_pages and n_b % nq == 0
    c_dim, r_dim = KV_LORA_RANK, QK_ROPE_DIM
    pt_flat = page_table.reshape(-1)
    n_pg = nq * g
    n_total = (n_b // nq) * n_steps

    def body(pt_ref, ql_ref, qr_ref, cn_ref, kn_ref, clat_ref, ckr_ref, o_ref,
             cbuf_ref, kbuf_ref, sems, m_ref, d_ref, acc_ref):
        s_idx = pl.program_id(1)
        step = pl.program_id(0) * n_steps + s_idx
        slot = step % 2
        nt = (((1,), (1,)), ((), ()))

        def page_copies(step_idx, slot_idx):
            b0 = (step_idx // n_steps) * nq
            s0 = (step_idx % n_steps) * g
            out = []
            for q in range(nq):
                for t in range(g):
                    page = pt_ref[(b0 + q) * n_pages + s0 + t]
                    out.append(pltpu.make_async_copy(
                        clat_ref.at[l, page], cbuf_ref.at[slot_idx, q * g + t], sems.at[slot_idx]))
                    out.append(pltpu.make_async_copy(
                        ckr_ref.at[l, page], kbuf_ref.at[slot_idx, q * g + t], sems.at[slot_idx]))
            return out

        @pl.when(step == 0)
        def _():
            for cp in page_copies(0, 0):
                cp.start()

        @pl.when(step + 1 < n_total)
        def _():
            for cp in page_copies(step + 1, 1 - slot):
                cp.start()

        for cp in page_copies(step, slot):
            cp.wait()
        c_refs = [cbuf_ref.at[slot, i] for i in range(n_pg)]
        kr_refs = [kbuf_ref.at[slot, i] for i in range(n_pg)]

        @pl.when(s_idx == 0)
        def _():
            for q in range(nq):
                cn = cn_ref[q]
                s_new = (jnp.sum(ql_ref[q] * cn, axis=-1, keepdims=True)
                         + jnp.sum(qr_ref[q][:, :r_dim] * kn_ref[q][:, :r_dim], axis=-1, keepdims=True))
                m_ref[q] = s_new * ATTN_SCALE
                d_ref[q] = jnp.ones((N_HEADS, 1), F32)
                acc_ref[q] = jnp.broadcast_to(cn, (N_HEADS, c_dim))

        for q in range(nq):
            qlb = ql_ref[q].astype(BF16)
            qrb = qr_ref[q][:, :r_dim].astype(BF16)
            cs = [c_refs[q * g + t][...].astype(BF16) for t in range(g)]
            s = jnp.concatenate(
                [lax.dot_general(qlb, cs[t], nt, preferred_element_type=F32)
                 + jnp.dot(qrb, kr_refs[q * g + t][...].astype(BF16), preferred_element_type=F32)
                 for t in range(g)], axis=1) * ATTN_SCALE
            m_old = m_ref[q]
            m_new = jnp.maximum(m_old, jnp.max(s, axis=-1, keepdims=True))
            scale_old = jnp.exp(m_old - m_new)
            p = jnp.exp(s - m_new)
            d_ref[q] = d_ref[q] * scale_old + jnp.sum(p, axis=-1, keepdims=True)
            pb = p.astype(BF16)
            pv = jnp.dot(pb[:, 0:PAGE_SIZE], cs[0], preferred_element_type=F32)
            for t in range(1, g):
                pv = pv + jnp.dot(pb[:, t * PAGE_SIZE:(t + 1) * PAGE_SIZE], cs[t],
                                  preferred_element_type=F32)
            acc_ref[q] = acc_ref[q] * scale_old + pv
            m_ref[q] = m_new

        @pl.when(s_idx == n_steps - 1)
        def _():
            o_ref[...] = acc_ref[...] / d_ref[...]

    def seq_spec(rows, width):
        return pl.BlockSpec((nq, rows, width), lambda b, s, pt: (b, 0, 0))

    in_specs = [seq_spec(N_HEADS, c_dim), seq_spec(N_HEADS, LANES), seq_spec(1, c_dim), seq_spec(1, LANES),
                pl.BlockSpec(memory_space=pl.ANY), pl.BlockSpec(memory_space=pl.ANY)]
    grid_spec = pltpu.PrefetchScalarGridSpec(
        num_scalar_prefetch=1, grid=(n_b // nq, n_steps), in_specs=in_specs,
        out_specs=seq_spec(N_HEADS, c_dim),
        scratch_shapes=[pltpu.VMEM((2, n_pg, PAGE_SIZE, c_dim), F32),
                        pltpu.VMEM((2, n_pg, r_dim, PAGE_SIZE), F32),
                        pltpu.SemaphoreType.DMA((2,)),
                        pltpu.VMEM((nq, N_HEADS, 1), F32), pltpu.VMEM((nq, N_HEADS, 1), F32),
                        pltpu.VMEM((nq, N_HEADS, c_dim), F32)])
    return pl.pallas_call(body, grid_spec=grid_spec,
                          out_shape=jax.ShapeDtypeStruct((n_b, N_HEADS, c_dim), F32),
                          compiler_params=_params(("arbitrary", "arbitrary")), name=name)(
        pt_flat, q_lat, q_rope, c_new, kr_new, cache_lat, cache_kr)


def _dwconv_prompt(u, w_dw, b_dw, g_n, b_n, *, n_batch, seq, tm, name):
    n_ch = u.shape[1]
    n_cc = n_ch // LANES
    nt = seq // tm
    halo_blocks = tm // CONV_HALO
    rc = 64
    first = CONV_HALO - (CONV_WIDTH - 1)

    def body(u_ref, halo_ref, w_ref, bd_ref, g_ref, b_ref, o_ref, ext_ref, y_ref):
        t, c = pl.program_id(1), pl.program_id(2)
        ext_ref[0:CONV_HALO, :] = jnp.where(t == 0, 0.0, halo_ref[...])
        ext_ref[CONV_HALO:, :] = u_ref[...]
        taps = [w_ref[k:k + 1, :] for k in range(CONV_WIDTH)]
        for r0 in range(0, tm, rc):
            acc = jnp.zeros((rc, LANES), F32)
            for k in range(CONV_WIDTH):
                acc = acc + taps[k] * ext_ref[r0 + first + k:r0 + first + k + rc, :]
            y_ref[c, r0:r0 + rc, :] = acc + bd_ref[...]

        @pl.when(c == n_cc - 1)
        def _():
            y = y_ref[...]
            mu = jnp.sum(jnp.sum(y, axis=0), axis=-1, keepdims=True) / n_ch
            yc = y - mu[None]
            var = jnp.sum(jnp.sum(yc * yc, axis=0), axis=-1, keepdims=True) / n_ch
            inv = lax.rsqrt(var + LN_EPS)
            for cc in range(n_cc):
                cols = slice(cc * LANES, (cc + 1) * LANES)
                z = yc[cc] * inv * g_ref[:, cols] + b_ref[:, cols]
                o_ref[:, cols] = (z * jax.nn.sigmoid(z)).astype(o_ref.dtype)

    def halo_imap(b, t, c):
        return (jnp.maximum((b * seq + t * tm) // CONV_HALO - 1, 0), c)

    in_specs = [
        pl.BlockSpec((tm, LANES), lambda b, t, c: (b * nt + t, c)),
        pl.BlockSpec((CONV_HALO, LANES), halo_imap),
        pl.BlockSpec((CONV_WIDTH, LANES), lambda b, t, c: (0, c)),
        pl.BlockSpec((1, LANES), lambda b, t, c: (0, c)),
        pl.BlockSpec((1, n_ch), lambda b, t, c: (0, 0)),
        pl.BlockSpec((1, n_ch), lambda b, t, c: (0, 0)),
    ]
    del halo_blocks
    return pl.pallas_call(
        body, grid=(n_batch, nt, n_cc), in_specs=in_specs,
        out_specs=pl.BlockSpec((tm, n_ch), lambda b, t, c: (b * nt + t, 0)),
        out_shape=jax.ShapeDtypeStruct(u.shape, BF16),
        scratch_shapes=[pltpu.VMEM((tm + CONV_HALO, LANES), F32), pltpu.VMEM((n_cc, tm, LANES), F32)],
        compiler_params=_params(("parallel", "parallel", "arbitrary")), name=name)(
        u, u, w_dw, b_dw, g_n, b_n)


def _dwconv_decode(u, buf, w_dw, b_dw, g_n, b_n, *, name):
    n_b, n_ch = u.shape
    bt = 8

    def body(u_ref, buf_ref, w_ref, bd_ref, g_ref, b_ref, o_ref):
        w = w_ref[...]
        y = jnp.sum(buf_ref[...] * w[None, :CONV_WIDTH - 1, :], axis=1)
        y = y + u_ref[...] * w[CONV_WIDTH - 1:CONV_WIDTH, :] + bd_ref[...]
        z = _ln_rows(y, g_ref[...], b_ref[...])
        o_ref[...] = (z * jax.nn.sigmoid(z)).astype(o_ref.dtype)

    vec = pl.BlockSpec((1, n_ch), lambda i: (0, 0))
    return pl.pallas_call(
        body, grid=(n_b // bt,),
        in_specs=[pl.BlockSpec((bt, n_ch), lambda i: (i, 0)),
                  pl.BlockSpec((bt, CONV_WIDTH - 1, n_ch), lambda i: (i, 0, 0)),
                  pl.BlockSpec((CONV_WIDTH, n_ch), lambda i: (0, 0)), vec, vec, vec],
        out_specs=pl.BlockSpec((bt, n_ch), lambda i: (i, 0)),
        out_shape=jax.ShapeDtypeStruct((n_b, n_ch), BF16),
        compiler_params=_params(("parallel",)), name=name)(u, buf, w_dw, b_dw, g_n, b_n)


def _sgu_gate_prompt(uv, w_s, b_s_t, g_v, b_v, *, n_batch, seq, name):
    n_chunks = seq // CHUNK
    gd = SGU_GROUP_DIM

    def body(u_ref, v_ref, ws_ref, bs_ref, g_ref, b_ref, o_ref, vo_ref):
        vn = _ln_rows(v_ref[...], g_ref[...], b_ref[...])
        vo_ref[...] = vn
        vb = vn.astype(BF16)
        row = lax.broadcasted_iota(jnp.int32, (CHUNK, CHUNK), 0)
        col = lax.broadcasted_iota(jnp.int32, (CHUNK, CHUNK), 1)
        keep = col <= row
        for grp in range(N_SGU_GROUPS):
            cols = slice(grp * gd, (grp + 1) * gd)
            wm = jnp.where(keep, ws_ref[grp], 0.0).astype(BF16)
            gate = jnp.dot(wm, vb[:, cols], preferred_element_type=F32) + bs_ref[:, grp:grp + 1]
            o_ref[:, cols] = (u_ref[:, cols] * gate).astype(o_ref.dtype)

    nb = D_SGU // D_SGU
    in_specs = [
        pl.BlockSpec((CHUNK, D_SGU), lambda b, n: (b * n_chunks + n, 0)),
        pl.BlockSpec((CHUNK, D_SGU), lambda b, n: (b * n_chunks + n, nb)),
        pl.BlockSpec((N_SGU_GROUPS, CHUNK, CHUNK), lambda b, n: (0, 0, 0)),
        pl.BlockSpec((CHUNK, N_SGU_GROUPS), lambda b, n: (0, 0)),
        pl.BlockSpec((1, D_SGU), lambda b, n: (0, 0)),
        pl.BlockSpec((1, D_SGU), lambda b, n: (0, 0)),
    ]
    out_specs = [pl.BlockSpec((CHUNK, D_SGU), lambda b, n: (b * n_chunks + n, 0)),
                 pl.BlockSpec((None, CHUNK, D_SGU), lambda b, n: (b, 0, 0))]
    out_shape = [jax.ShapeDtypeStruct((n_batch * seq, D_SGU), BF16),
                 jax.ShapeDtypeStruct((n_batch, CHUNK, D_SGU), F32)]
    return pl.pallas_call(body, grid=(n_batch, n_chunks), in_specs=in_specs, out_specs=out_specs,
                          out_shape=out_shape, compiler_params=_params(("parallel", "arbitrary")),
                          name=name)(uv, uv, w_s, b_s_t, g_v, b_v)


def _sgu_gate_decode(uv, w_vec, b_vec, g_v, b_v, *, name):
    n_b = uv.shape[0]

    def body(u_ref, v_ref, w_ref, bs_ref, g_ref, b_ref, o_ref, vo_ref):
        vn = _ln_rows(v_ref[...], g_ref[...], b_ref[...])
        vo_ref[...] = vn
        o_ref[...] = (u_ref[...] * (vn * w_ref[...] + bs_ref[...])).astype(o_ref.dtype)

    vec = pl.BlockSpec((1, D_SGU), lambda i: (0, 0))
    return pl.pallas_call(
        body, grid=(1,),
        in_specs=[pl.BlockSpec((n_b, D_SGU), lambda i: (0, 0)), pl.BlockSpec((n_b, D_SGU), lambda i: (0, 1)),
                  vec, vec, vec, vec],
        out_specs=[pl.BlockSpec((n_b, D_SGU), lambda i: (0, 0)), pl.BlockSpec((n_b, D_SGU), lambda i: (0, 0))],
        out_shape=[jax.ShapeDtypeStruct((n_b, D_SGU), BF16), jax.ShapeDtypeStruct((n_b, D_SGU), F32)],
        compiler_params=_params(("arbitrary",)), name=name)(uv, uv, w_vec, b_vec, g_v, b_v)


def _router(x, mod, w_r, b_r, l, *, tm, name):
    m_rows = x.shape[0]
    marr, rows_per_b = mod
    r = marr.shape[1]
    n_e = N_EXPERTS

    def body(x_ref, sh_ref, sc_ref, w_ref, b_ref, h_ref, gates_ref, idx_ref):
        h = x_ref[...] * (1.0 + sc_ref[...]) + sh_ref[...]
        h_ref[...] = h
        logits = jnp.dot(h, w_ref[...], preferred_element_type=F32,
                         precision=lax.Precision.HIGHEST) + b_ref[...]
        lane = lax.broadcasted_iota(jnp.int32, logits.shape, 1)
        m1 = jnp.max(logits, axis=-1, keepdims=True)
        i1 = jnp.min(jnp.where(logits == m1, lane, n_e), axis=-1, keepdims=True)
        rest = jnp.where(lane == i1, -jnp.inf, logits)
        m2 = jnp.max(rest, axis=-1, keepdims=True)
        i2 = jnp.min(jnp.where(rest == m2, lane, n_e), axis=-1, keepdims=True)
        e2 = jnp.exp(m2 - m1)
        den = 1.0 + e2
        gates_ref[...] = jnp.where(lane == i1, 1.0 / den, 0.0) + jnp.where(lane == i2, e2 / den, 0.0)
        idx_ref[...] = jnp.where(lane == 0, i1, jnp.where(lane == 1, i2, 0))

    in_specs = [
        pl.BlockSpec((tm, D_MODEL), lambda i: (i, 0)),
        pl.BlockSpec((None, r, D_MODEL), lambda i: ((i * tm) // rows_per_b, 0, 3)),
        pl.BlockSpec((None, r, D_MODEL), lambda i: ((i * tm) // rows_per_b, 0, 4)),
        pl.BlockSpec((None, D_MODEL, n_e), lambda i: (l, 0, 0)),
        pl.BlockSpec((None, 1, n_e), lambda i: (l, 0, 0)),
    ]
    out_specs = [pl.BlockSpec((tm, D_MODEL), lambda i: (i, 0)),
                 pl.BlockSpec((tm, n_e), lambda i: (i, 0)),
                 pl.BlockSpec((tm, n_e), lambda i: (i, 0))]
    out_shape = [jax.ShapeDtypeStruct((m_rows, D_MODEL), F32),
                 jax.ShapeDtypeStruct((m_rows, n_e), F32),
                 jax.ShapeDtypeStruct((m_rows, n_e), jnp.int32)]
    return pl.pallas_call(body, grid=(m_rows // tm,), in_specs=in_specs, out_specs=out_specs,
                          out_shape=out_shape, compiler_params=_params(("parallel",)), name=name)(
        x, marr, marr, w_r, b_r)


def _gather_rows(h, row_src, n_used, *, tm, name):
    n_rows = row_src.shape[0]
    d = h.shape[1]

    def body(rs_ref, nu_ref, h_ref, o_ref, buf_ref, sems):
        i = pl.program_id(0)
        slot = i % 2

        def row_copy(tile_idx, slot_idx, r):
            src = rs_ref[tile_idx * tm + r]
            return pltpu.make_async_copy(h_ref.at[pl.ds(src, 1)], buf_ref.at[slot_idx, pl.ds(r, 1)],
                                         sems.at[slot_idx])

        def issue(tile_idx, slot_idx):
            def f(r, c):
                row_copy(tile_idx, slot_idx, r).start()
                return c
            lax.fori_loop(0, tm, f, 0, unroll=8)

        @pl.when(i == 0)
        def _():
            issue(0, 0)

        @pl.when(i + 1 < nu_ref[0])
        def _():
            issue(i + 1, 1 - slot)

        @pl.when(i < nu_ref[0])
        def _():
            def wait(r, c):
                row_copy(i, slot, r).wait()
                return c

            lax.fori_loop(0, tm, wait, 0, unroll=8)
            o_ref[...] = buf_ref[slot].astype(o_ref.dtype)

        @pl.when(i >= nu_ref[0])
        def _():
            o_ref[...] = jnp.zeros(o_ref.shape, o_ref.dtype)

    grid_spec = pltpu.PrefetchScalarGridSpec(
        num_scalar_prefetch=2, grid=(n_rows // tm,), in_specs=[pl.BlockSpec(memory_space=pl.ANY)],
        out_specs=pl.BlockSpec((tm, d), lambda i, rs, nu: (i, 0)),
        scratch_shapes=[pltpu.VMEM((2, tm, d), h.dtype), pltpu.SemaphoreType.DMA((2,))])
    return pl.pallas_call(body, grid_spec=grid_spec, out_shape=jax.ShapeDtypeStruct((n_rows, d), BF16),
                          compiler_params=_params(("arbitrary",)), name=name)(row_src, n_used, h)


def _moe_combine(expert_out, pos, wts, x, mod, ln_g, ln_b, ln_idx, *, tm, name):
    m_rows = x.shape[0]
    marr, rows_per_b = mod
    r = marr.shape[1]

    def body(pos_ref, eo_ref, w_ref, x_ref, gt_ref, g_ref, b_ref, o_ref, buf_ref, sem):
        base = pl.program_id(0) * tm

        def issue(t, c):
            for k in range(TOP_K):
                src = pos_ref[(base + t) * TOP_K + k]
                pltpu.make_async_copy(eo_ref.at[pl.ds(src, 1)], buf_ref.at[k, pl.ds(t, 1)], sem).start()
            return c

        lax.fori_loop(0, tm, issue, 0, unroll=4)

        def wait(t, c):
            for k in range(TOP_K):
                pltpu.make_async_copy(eo_ref.at[pl.ds(0, 1)], buf_ref.at[k, pl.ds(t, 1)], sem).wait()
            return c

        lax.fori_loop(0, tm, wait, 0, unroll=4)
        w = w_ref[...]
        f = w[:, 0:1] * buf_ref[0] + w[:, 1:2] * buf_ref[1]
        o_ref[...] = _ln_rows(ALPHA * x_ref[...] + (1.0 + gt_ref[...]) * f, g_ref[...], b_ref[...])

    in_specs = [
        pl.BlockSpec(memory_space=pl.ANY),
        pl.BlockSpec((tm, TOP_K), lambda i, p: (i, 0)),
        pl.BlockSpec((tm, D_MODEL), lambda i, p: (i, 0)),
        pl.BlockSpec((None, r, D_MODEL), lambda i, p: ((i * tm) // rows_per_b, 0, 5)),
        pl.BlockSpec((None, 1, D_MODEL), lambda i, p: (ln_idx, 0, 0)),
        pl.BlockSpec((None, 1, D_MODEL), lambda i, p: (ln_idx, 0, 0)),
    ]
    grid_spec = pltpu.PrefetchScalarGridSpec(
        num_scalar_prefetch=1, grid=(m_rows // tm,), in_specs=in_specs,
        out_specs=pl.BlockSpec((tm, D_MODEL), lambda i, p: (i, 0)),
        scratch_shapes=[pltpu.VMEM((TOP_K, tm, D_MODEL), F32), pltpu.SemaphoreType.DMA(())])
    return pl.pallas_call(body, grid_spec=grid_spec, out_shape=jax.ShapeDtypeStruct(x.shape, F32),
                          compiler_params=_params(("parallel",)), name=name)(
        pos, expert_out, wts, x, marr, ln_g, ln_b)


def _route_tables(idx, n_tokens, tm):
    n_pairs = n_tokens * TOP_K
    n_tiles = -(-(n_pairs + N_EXPERTS * (tm - 1)) // tm)
    n_tiles = -(-n_tiles // ROW_TILE_GROUP) * ROW_TILE_GROUP
    e_flat = idx[:, :TOP_K].reshape(-1)
    onehot = (e_flat[:, None] == jnp.arange(N_EXPERTS, dtype=jnp.int32)[None, :]).astype(jnp.int32)
    rank = jnp.take_along_axis(jnp.cumsum(onehot, axis=0), e_flat[:, None], axis=1)[:, 0] - 1
    counts = jnp.sum(onehot, axis=0)
    tiles_per = (counts + tm - 1) // tm
    tile_end = jnp.cumsum(tiles_per)
    tile_start = tile_end - tiles_per
    pos = (tile_start[e_flat] * tm + rank).astype(jnp.int32)
    row_src = jnp.zeros((n_tiles * tm,), jnp.int32).at[pos].set(
        jnp.arange(n_pairs, dtype=jnp.int32) // TOP_K)
    n_used = tile_end[-1].astype(jnp.int32)
    tile_ids = jnp.arange(n_tiles, dtype=jnp.int32)
    tile_e = jnp.sum((tile_ids[:, None] >= tile_end[None, :]).astype(jnp.int32), axis=1)
    last_e = jnp.sum((n_used - 1 >= tile_end).astype(jnp.int32))
    tile_e = jnp.where(tile_ids < n_used, tile_e, last_e).astype(jnp.int32)
    return row_src, tile_e, n_used.reshape(1), pos


def _rope_tables(pos, width_blocks):
    half = QK_ROPE_DIM // 2
    inv = ROPE_THETA ** (-jnp.arange(half, dtype=F32) / half)
    ang = pos.astype(F32)[:, None] * inv[None, :]
    zeros = jnp.zeros((pos.shape[0], LANES - QK_ROPE_DIM), F32)
    ta = jnp.concatenate([jnp.cos(ang), jnp.cos(ang), zeros], axis=1)
    tb = jnp.concatenate([jnp.sin(ang), jnp.sin(ang), zeros], axis=1)
    return jnp.tile(ta, (1, width_blocks)), jnp.tile(tb, (1, width_blocks))


def _rot_cols(w):
    half = QK_ROPE_DIM // 2
    return jnp.concatenate([-w[..., half:], w[..., :half]], axis=-1)


def _pad_lanes(w):
    return jnp.concatenate([w, jnp.zeros(w.shape[:-1] + (LANES - w.shape[-1],), w.dtype)], axis=-1)


def kernel(x_prompt, x_sample, cache_latent, cache_krope, state_conv, page_table, c_prompt, c_sample,
           w_ada, b_ada, ln_g, ln_b,
           mla_w_dqkv, mla_g_q, mla_g_kv, mla_w_uq, mla_w_qr, mla_w_uk, mla_w_uv, mla_w_o,
           conv_w_pw1, conv_b_pw1, conv_w_dw, conv_b_dw, conv_g_n, conv_b_n, conv_w_pw2, conv_b_pw2,
           sgu_w_in, sgu_b_in, sgu_g_v, sgu_b_v, sgu_w_s, sgu_b_s, sgu_w_out, sgu_b_out,
           ffn_w_gu, ffn_w_down, moe_w_r, moe_b_r, moe_w_gu, moe_w_down):
    n_batch, seq, d = x_prompt.shape
    n_dec = x_sample.shape[0]
    assert x_sample.shape[1] == 1 and d == D_MODEL
    past = page_table.shape[1] * cache_latent.shape[2]
    t_p = n_batch * seq
    tm_p = min(512, seq)
    tm_ff = min(1024, seq)
    tn_up = 1024
    tm_moe = min(512, seq)

    n_c = n_batch + n_dec
    n_c_pad = -(-n_c // 16) * 16
    c_all = jnp.concatenate([c_prompt, c_sample, jnp.zeros((n_c_pad - n_c, d), F32)], axis=0)
    b_ada3 = b_ada.reshape(DEPTH, 1, N_MOD * d)
    mods_p, mods_s = [], []
    for l in range(DEPTH):
        tn = 1024
        (mod_l,) = _proj(
            c_all, [(w_ada, (l,), 0)], lambda ys, es: ys,
            [((n_c_pad, N_MOD * d), F32, (n_c_pad, tn), lambda i, j: (0, j))],
            tm=n_c_pad, tn=tn, nj=N_MOD * d // tn, bs=[(b_ada3, l, 0)], silu_in=True, name=f"ada{l}")
        mods_p.append(mod_l[:n_batch].reshape(n_batch, 1, N_MOD * d))
        mods_s.append(mod_l[n_batch:n_c].reshape(1, n_dec, N_MOD * d))

    ln_g3 = ln_g.reshape(DEPTH * 2, 1, d)
    ln_b3 = ln_b.reshape(DEPTH * 2, 1, d)

    n_a = mla_w_dqkv.shape[0]
    hq = N_HEADS * QK_NOPE_DIM
    w_uq2 = mla_w_uq.reshape(n_a, Q_LORA_RANK, hq)
    w_uk2 = mla_w_uk.reshape(n_a, KV_LORA_RANK, hq)
    w_uv2 = mla_w_uv.reshape(n_a, KV_LORA_RANK, N_HEADS * V_HEAD_DIM)
    w_qr_a = _pad_lanes(mla_w_qr).reshape(n_a, Q_LORA_RANK, N_HEADS * LANES)
    w_qr_b = _pad_lanes(_rot_cols(mla_w_qr)).reshape(n_a, Q_LORA_RANK, N_HEADS * LANES)
    w_kr = mla_w_dqkv[:, :, Q_LORA_RANK + KV_LORA_RANK:]
    w_kr_a = _pad_lanes(w_kr)
    w_kr_b = _pad_lanes(_rot_cols(w_kr))
    g_q3 = mla_g_q.reshape(n_a, 1, Q_LORA_RANK)
    g_kv3 = mla_g_kv.reshape(n_a, 1, KV_LORA_RANK)
    n_ff = ffn_w_gu.shape[0]
    n_moe = moe_w_r.shape[0]
    b_r3 = moe_b_r.reshape(n_moe, 1, N_EXPERTS)

    def mla_front(x, mod, rows_per_b, tm_down, tm, tabs, tabs_wide, j, tag):
        ta, tb = tabs
        c_q, c_kv, kr = _mla_down(x, (mod, rows_per_b), mla_w_dqkv, j, w_kr_a[j], w_kr_b[j],
                                  g_q3, g_kv3, ta, tb, tm=tm_down, name=f"mla_down_{tag}")
        m_rows = x.shape[0]
        tn = tn_up
        n_tab = ta.shape[0] // tm
        (qn,) = _proj(c_q, [(w_uq2, (j,), 0)], lambda ys, es: ys,
                      [((m_rows, hq), BF16 if tag[0] == 'p' else F32, (tm, tn), lambda i, jj: (i, jj))],
                      tm=tm, tn=tn, nj=hq // tn, name=f"mla_qn_{tag}")
        taw, tbw = tabs_wide
        (qr,) = _proj(c_q, [(w_qr_a, (j,), 0), (w_qr_b, (j,), 0)],
                      lambda ys, es: [ys[0] * es[0] + ys[1] * es[1]],
                      [((m_rows, hq), BF16 if tag[0] == 'p' else F32, (tm, tn), lambda i, jj: (i, jj))],
                      tm=tm, tn=tn, nj=hq // tn,
                      extras=[(taw, (tm, tn), lambda i, jj: (i % n_tab, 0)),
                              (tbw, (tm, tn), lambda i, jj: (i % n_tab, 0))],
                      name=f"mla_qr_{tag}")
        return c_q, c_kv, kr, qn, qr

    def grp(n_row_tiles, want):
        return max(c for c in range(1, want + 1) if n_row_tiles % c == 0)

    def res_args(x, mod, comp, rows_per_b, l, s):
        return (x, mod, comp, rows_per_b, ln_g3, ln_b3, 2 * l + s)

    def dense_ffn(x, mod, rows_per_b, tm_a, tm_b, l, tag):
        jf = l // 2
        tn = 256
        nb = DENSE_FF // tn
        (act,) = _proj(x, [(ffn_w_gu, (jf,), 0), (ffn_w_gu, (jf,), nb)],
                       lambda ys, es: [ys[0] * jax.nn.sigmoid(ys[0]) * ys[1]],
                       [((x.shape[0], DENSE_FF), BF16, (tm_a, tn), lambda i, jj: (i, jj))],
                       tm=tm_a, tn=tn, nj=nb, mod=(mod, 3, 4, rows_per_b), name=f"ffn_gu_{tag}")
        return _proj_k(act, ffn_w_down, (jf,), tm=tm_b, tk=512, group=grp(x.shape[0] // tm_b, 4),
                       res=res_args(x, mod, 5, rows_per_b, l, 1), name=f"ffn_down_{tag}")

    pos_p = jnp.arange(seq, dtype=jnp.int32)
    pos_s = jnp.full((n_dec,), past, dtype=jnp.int32)
    tabs_p, tabs_s = _rope_tables(pos_p, 1), _rope_tables(pos_s, 1)
    tabs_pw, tabs_sw = _rope_tables(pos_p, tn_up // LANES), _rope_tables(pos_s, tn_up // LANES)

    xp = x_prompt.reshape(t_p, d)
    xs = x_sample.reshape(n_dec, d)
    cache_kr_t = jnp.swapaxes(cache_krope, 2, 3)
    lat_p, kr_p, lat_s, kr_s = [], [], [], []
    conv_p = conv_s = v_p = v_s = None

    for l in range(DEPTH):
        kind, j = l % 3, l // 3
        mp, ms = mods_p[l], mods_s[l]
        if kind == 0:
            c_q, c_kv, kr, qn, qr = mla_front(xp, mp, seq, tm_p, tm_ff, tabs_p, tabs_pw, j, f"p{l}")
            tn = tn_up
            (kn,) = _proj(c_kv, [(w_uk2, (j,), 0)], lambda ys, es: ys,
                          [((t_p, hq), BF16, (tm_ff, tn), lambda i, jj: (i, jj))],
                          tm=tm_ff, tn=tn, nj=hq // tn, name=f"mla_kn_p{l}")
            (vv,) = _proj(c_kv, [(w_uv2, (j,), 0)], lambda ys, es: ys,
                          [((t_p, hq), BF16, (tm_ff, tn), lambda i, jj: (i, jj))],
                          tm=tm_ff, tn=tn, nj=hq // tn, name=f"mla_v_p{l}")
            o = _attn_prompt(qn, qr, kn, kr, vv, n_batch=n_batch, seq=seq, tq=min(512, seq),
                             name=f"attn_p{l}")
            xp = _proj_k(o, mla_w_o, (j,), tm=tm_p, tk=1024, group=grp(t_p // tm_p, 2),
                         res=res_args(xp, mp, 2, seq, l, 0),
                         name=f"mla_o_p{l}")
            lat_p.append(c_kv.reshape(n_batch, seq, KV_LORA_RANK))
            kr_p.append(kr[:, :QK_ROPE_DIM].reshape(n_batch, seq, QK_ROPE_DIM))
            c_q, c_kv, kr, qn, qr = mla_front(xs, ms, n_dec, n_dec, n_dec, tabs_s, tabs_sw, j, f"s{l}")
            q_lat = _head_mix(qn, w_uk2, j, contract_w_last=True, name=f"mla_qlat_s{l}")
            o_lat = _attn_decode(q_lat.reshape(n_dec, N_HEADS, KV_LORA_RANK),
                                 qr.reshape(n_dec, N_HEADS, LANES),
                                 c_kv.reshape(n_dec, 1, KV_LORA_RANK), kr.reshape(n_dec, 1, LANES),
                                 cache_latent, cache_kr_t, page_table, j, name=f"attn_s{l}")
            o = _head_mix(o_lat.reshape(n_dec, N_HEADS * KV_LORA_RANK), w_uv2, j,
                          contract_w_last=False, name=f"mla_ov_s{l}")
            xs = _proj_k(o, mla_w_o, (j,), tm=n_dec, tk=1024, res=res_args(xs, ms, 2, n_dec, l, 0),
                         name=f"mla_o_s{l}")
            lat_s.append(c_kv.reshape(n_dec, 1, KV_LORA_RANK))
            kr_s.append(kr[:, :QK_ROPE_DIM].reshape(n_dec, 1, QK_ROPE_DIM))
        elif kind == 1:
            d_c = conv_w_pw1.shape[2] // 2
            b1 = conv_b_pw1.reshape(-1, 1, 2 * d_c)
            b2 = conv_b_pw2.reshape(-1, 1, d)
            tn = 256
            glu = lambda ys, es: [ys[0] * jax.nn.sigmoid(ys[1])]

            def pw1(x, mod, rows_per_b, tm, tag):
                return _proj(x, [(conv_w_pw1, (j,), 0), (conv_w_pw1, (j,), d_c // tn)], glu,
                             [((x.shape[0], d_c), F32, (tm, tn), lambda i, jj: (i, jj))],
                             tm=tm, tn=tn, nj=d_c // tn, bs=[(b1, j, 0), (b1, j, d_c // tn)],
                             mod=(mod, 0, 1, rows_per_b), name=f"conv_pw1_{tag}")[0]

            u = pw1(xp, mp, seq, tm_ff, f"p{l}")
            y = _dwconv_prompt(u, conv_w_dw[j], conv_b_dw[j].reshape(1, d_c), conv_g_n[j].reshape(1, d_c),
                               conv_b_n[j].reshape(1, d_c), n_batch=n_batch, seq=seq, tm=tm_p,
                               name=f"dwconv_p{l}")
            xp = _proj_k(y, conv_w_pw2, (j,), tm=tm_p, tk=1024, group=grp(t_p // tm_p, 2), bias=(b2, j),
                         res=res_args(xp, mp, 2, seq, l, 0), name=f"conv_pw2_p{l}")
            conv_p = u.reshape(n_batch, seq, d_c)[:, seq - (CONV_WIDTH - 1):][None]

            u = pw1(xs, ms, n_dec, n_dec, f"s{l}")
            y = _dwconv_decode(u, state_conv[j], conv_w_dw[j], conv_b_dw[j].reshape(1, d_c),
                               conv_g_n[j].reshape(1, d_c), conv_b_n[j].reshape(1, d_c), name=f"dwconv_s{l}")
            xs = _proj_k(y, conv_w_pw2, (j,), tm=n_dec, tk=1024, bias=(b2, j),
                         res=res_args(xs, ms, 2, n_dec, l, 0), name=f"conv_pw2_s{l}")
            conv_s = jnp.concatenate([state_conv[j][:, 1:], u[:, None, :]], axis=1)[None]
        else:
            b_in = sgu_b_in.reshape(-1, 1, 2 * D_SGU)
            b_out = sgu_b_out.reshape(-1, 1, d)
            tn = 512
            gelu = lambda ys, es: [_gelu_tanh(ys[0])]

            def sgu_in(x, mod, rows_per_b, tm, tag):
                return _proj(x, [(sgu_w_in, (j,), 0)], gelu,
                             [((x.shape[0], 2 * D_SGU), F32, (tm, tn), lambda i, jj: (i, jj))],
                             tm=tm, tn=tn, nj=2 * D_SGU // tn, bs=[(b_in, j, 0)],
                             mod=(mod, 0, 1, rows_per_b), name=f"sgu_in_{tag}")[0]

            g_v = sgu_g_v[j].reshape(1, D_SGU)
            b_v = sgu_b_v[j].reshape(1, D_SGU)
            uv = sgu_in(xp, mp, seq, tm_ff, f"p{l}")
            ug, v_last = _sgu_gate_prompt(uv, sgu_w_s[j], sgu_b_s[j].T, g_v, b_v, n_batch=n_batch, seq=seq,
                                          name=f"sgu_gate_p{l}")
            xp = _proj_k(ug, sgu_w_out, (j,), tm=tm_p, tk=1024, group=grp(t_p // tm_p, 2), bias=(b_out, j),
                         res=res_args(xp, mp, 2, seq, l, 0), name=f"sgu_out_p{l}")
            v_p = v_last[None]

            uv = sgu_in(xs, ms, n_dec, n_dec, f"s{l}")
            w_vec = jnp.repeat(sgu_w_s[j][:, 0, 0], SGU_GROUP_DIM).reshape(1, D_SGU)
            b_vec = jnp.repeat(sgu_b_s[j][:, 0], SGU_GROUP_DIM).reshape(1, D_SGU)
            ug, vn = _sgu_gate_decode(uv, w_vec, b_vec, g_v, b_v, name=f"sgu_gate_s{l}")
            xs = _proj_k(ug, sgu_w_out, (j,), tm=n_dec, tk=1024, bias=(b_out, j),
                         res=res_args(xs, ms, 2, n_dec, l, 0), name=f"sgu_out_s{l}")
            v_s = vn.reshape(n_dec, 1, D_SGU)[None]

        if l % 2 == 0:
            xp = dense_ffn(xp, mp, seq, tm_ff, tm_p, l, f"p{l}")
            xs = dense_ffn(xs, ms, n_dec, n_dec, n_dec, l, f"s{l}")
        else:
            jm = l // 2
            h_p, gates_p, idx_p = _router(xp, (mp, seq), moe_w_r, b_r3, jm, tm=tm_p, name=f"router_p{l}")
            h_s, gates_s, idx_s = _router(xs, (ms, n_dec), moe_w_r, b_r3, jm, tm=n_dec, name=f"router_s{l}")
            h_all = jnp.concatenate([h_p, h_s], axis=0)
            idx_all = jnp.concatenate([idx_p, idx_s], axis=0)
            gates_all = jnp.concatenate([gates_p, gates_s], axis=0)
            n_tok = t_p + n_dec
            row_src, tile_e, n_used, pos = _route_tables(idx_all, n_tok, tm_moe)
            wts = jnp.take_along_axis(gates_all, idx_all[:, :TOP_K], axis=1)
            n_rows = row_src.shape[0]
            tn = 512
            nb = EXPERT_FF // tn
            h_sorted = _gather_rows(h_all, row_src, n_used, tm=tm_moe, name=f"moe_gather{l}")
            (act,) = _proj(h_sorted, [(moe_w_gu, (jm, 'e'), 0), (moe_w_gu, (jm, 'e'), nb)],
                           lambda ys, es: [ys[0] * jax.nn.sigmoid(ys[0]) * ys[1]],
                           [((n_rows, EXPERT_FF), BF16, (tm_moe, tn), lambda i, jj: (i, jj))],
                           tm=tm_moe, tn=tn, nj=nb, expert=(tile_e, n_used), cols_outer=True,
                           name=f"moe_gu{l}")
            eo = _proj_k(act, moe_w_down, (jm, 'e'), tm=tm_moe, tk=1024, group=ROW_TILE_GROUP,
                         expert=(tile_e, n_used),
                         name=f"moe_down{l}")
            xp = _moe_combine(eo, pos[:t_p * TOP_K], wts[:t_p], xp, (mp, seq), ln_g3, ln_b3, 2 * l + 1,
                              tm=tm_p, name=f"moe_comb_p{l}")
            xs = _moe_combine(eo, pos[t_p * TOP_K:], wts[t_p:], xs, (ms, n_dec), ln_g3, ln_b3, 2 * l + 1,
                              tm=n_dec, name=f"moe_comb_s{l}")

    return (xp.reshape(n_batch, seq, d), xs.reshape(n_dec, 1, d),
            jnp.stack(lat_p), jnp.stack(kr_p), conv_p, v_p,
            jnp.stack(lat_s), jnp.stack(kr_s), conv_s, v_s)
```
